```python
import jax, jax.numpy as jnp
from jax import lax
import numpy as np

D_MODEL = 2048
BATCH = 16
SEQ = 2048
DEPTH = 2

HEAD_DIM = 128
N_Q_GLOBAL = 8
N_KV_GLOBAL = 2
N_Q_LOCAL = 8
N_KV_LOCAL = 2
QA = N_Q_GLOBAL * HEAD_DIM
KA = N_KV_GLOBAL * HEAD_DIM
QB = N_Q_LOCAL * HEAD_DIM
KB = N_KV_LOCAL * HEAD_DIM
MIX_WIDTH = QA + QB
IN_COLS = QA + 2 * KA + QB + 2 * KB
WINDOW = 128
BLOCK_Q = 128
GRID_W = 64
ROPE_THETA = 10000.0
FFN_DIM = 5632
N_EXPERTS = 8
TOP_K = 2
EPS = 1e-6
N_DENSE = (DEPTH + 1) // 2
N_MOE = DEPTH // 2

kernel_name = "hybrid_parallel_group_encoder"


def rms_norm(x, g):
    xf = x.astype(jnp.float32)
    y = xf * lax.rsqrt(jnp.mean(xf * xf, axis=-1, keepdims=True) + EPS)
    return (y * g.astype(jnp.float32)).astype(x.dtype)


def rope_angles(pos, dim):
    inv = 1.0 / (ROPE_THETA ** (jnp.arange(0, dim, 2, dtype=jnp.float32) / dim))
    return pos.astype(jnp.float32)[:, None] * inv[None, :]


def apply_rope(x, ang):
    half = x.shape[-1] // 2
    cos = jnp.cos(ang)[None, :, None, :]
    sin = jnp.sin(ang)[None, :, None, :]
    x1 = x[..., :half].astype(jnp.float32)
    x2 = x[..., half:].astype(jnp.float32)
    return jnp.concatenate([x1 * cos - x2 * sin, x2 * cos + x1 * sin], axis=-1).astype(x.dtype)


def apply_axial_rope(x, ang_row, ang_col):
    half = x.shape[-1] // 2
    return jnp.concatenate([apply_rope(x[..., :half], ang_row),
                            apply_rope(x[..., half:], ang_col)], axis=-1)


def global_attention(q, k, v):
    B, S, Hq, Dh = q.shape
    Hkv = k.shape[2]
    G = Hq // Hkv
    nb = S // BLOCK_Q
    qb = q.reshape(B, nb, BLOCK_Q, Hkv, G, Dh).transpose(1, 0, 2, 3, 4, 5)
    scale = Dh ** -0.5

    def one_block(q_blk):
        s = jnp.einsum('bqkgd,bskd->bkgqs', q_blk, k,
                       preferred_element_type=jnp.float32) * scale
        p = jax.nn.softmax(s, axis=-1).astype(v.dtype)
        return jnp.einsum('bkgqs,bskd->bqkgd', p, v)

    o = lax.map(one_block, qb)
    return o.transpose(1, 0, 2, 3, 4, 5).reshape(B, S, Hq * Dh)


def window_attention(q, k, v, sinks):
    B, S, Hq, Dh = q.shape
    Hkv = k.shape[2]
    G = Hq // Hkv
    nb = S // BLOCK_Q
    pad = ((0, 0), (BLOCK_Q, BLOCK_Q), (0, 0), (0, 0))
    kp = jnp.pad(k, pad).reshape(B, nb + 2, BLOCK_Q, Hkv, Dh)
    vp = jnp.pad(v, pad).reshape(B, nb + 2, BLOCK_Q, Hkv, Dh)
    kw = jnp.concatenate([kp[:, :-2], kp[:, 1:-1], kp[:, 2:]], axis=2)
    vw = jnp.concatenate([vp[:, :-2], vp[:, 1:-1], vp[:, 2:]], axis=2)
    qb = q.reshape(B, nb, BLOCK_Q, Hkv, G, Dh)

    q_pos = jnp.arange(S).reshape(nb, BLOCK_Q)
    k_pos = jnp.arange(nb)[:, None] * BLOCK_Q - BLOCK_Q + jnp.arange(3 * BLOCK_Q)[None, :]
    diff = k_pos[:, None, :] - q_pos[:, :, None]
    mask = (jnp.abs(diff) <= WINDOW) & (k_pos[:, None, :] >= 0) & (k_pos[:, None, :] < S)

    scale = Dh ** -0.5
    s = jnp.einsum('bnqkgd,bnskd->bnkgqs', qb, kw,
                   preferred_element_type=jnp.float32) * scale
    s = jnp.where(mask[None, :, None, None], s, -jnp.inf)
    sink = sinks.astype(jnp.float32).reshape(1, 1, Hkv, G, 1, 1)
    m = jnp.maximum(jnp.max(s, axis=-1, keepdims=True), sink)
    e = jnp.exp(s - m)
    p = (e / (jnp.sum(e, axis=-1, keepdims=True) + jnp.exp(sink - m))).astype(v.dtype)
    o = jnp.einsum('bnkgqs,bnskd->bnqkgd', p, vw)
    return o.reshape(B, S, Hq * Dh)


def hybrid_mixer(h, w_in, q_norm_g, k_norm_g, sinks, grp_norm_a, grp_norm_b, w_out,
                 ang_row, ang_col, ang_seq):
    B, S, _ = h.shape
    proj = h @ w_in
    cuts = [QA, QA + KA, QA + 2 * KA, QA + 2 * KA + QB, QA + 2 * KA + QB + KB]
    qa, ka, va, qb, kb, vb = jnp.split(proj, cuts, axis=-1)

    qa = rms_norm(qa.reshape(B, S, N_Q_GLOBAL, HEAD_DIM), q_norm_g)
    ka = rms_norm(ka.reshape(B, S, N_KV_GLOBAL, HEAD_DIM), k_norm_g)
    va = va.reshape(B, S, N_KV_GLOBAL, HEAD_DIM)
    qa = apply_axial_rope(qa, ang_row, ang_col)
    ka = apply_axial_rope(ka, ang_row, ang_col)
    out_a = global_attention(qa, ka, va)

    qb = apply_rope(qb.reshape(B, S, N_Q_LOCAL, HEAD_DIM), ang_seq)
    kb = apply_rope(kb.reshape(B, S, N_KV_LOCAL, HEAD_DIM), ang_seq)
    vb = vb.reshape(B, S, N_KV_LOCAL, HEAD_DIM)
    out_b = window_attention(qb, kb, vb, sinks)

    merged = jnp.concatenate([rms_norm(out_a, grp_norm_a), rms_norm(out_b, grp_norm_b)], axis=-1)
    return merged @ w_out


def swiglu(t, w_gate, w_up, w_down):
    return (jax.nn.silu(t @ w_gate) * (t @ w_up)) @ w_down


def moe_swiglu(h, w_router, w_gate, w_up, w_down):
    B, S, D = h.shape
    t = h.reshape(B * S, D)
    logits = (t @ w_router).astype(jnp.float32)
    top_v, top_i = lax.top_k(logits, TOP_K)
    wts = jax.nn.softmax(top_v, axis=-1)
    combine = jnp.sum(jax.nn.one_hot(top_i, N_EXPERTS, dtype=jnp.float32) * wts[..., None], axis=1)
    out = jnp.zeros_like(t)
    for e in range(N_EXPERTS):
        out = out + combine[:, e:e + 1].astype(t.dtype) * swiglu(t, w_gate[e], w_up[e], w_down[e])
    return out.reshape(B, S, D)


def setup_inputs(seed: int = 0) -> dict:
    key = jax.random.key(seed)
    ks = jax.random.split(key, 22)
    D, F, E, L = D_MODEL, FFN_DIM, N_EXPERTS, DEPTH
    nrm = jax.random.normal
    gain = lambda k, shape: 1.0 + 0.1 * nrm(k, shape, jnp.float32)
    return {
        "x": nrm(ks[0], (BATCH, SEQ, D), jnp.float32),
        "c": nrm(ks[1], (BATCH, D), jnp.float32),
        "w_mod": nrm(ks[2], (L, D, 6 * D), jnp.float32) * D ** -0.5,
        "b_mod": 0.01 * nrm(ks[3], (L, 6 * D), jnp.float32),
        "pre_mix_g": gain(ks[4], (L, D)),
        "post_mix_g": gain(ks[5], (L, D)),
        "pre_ffn_g": gain(ks[6], (L, D)),
        "post_ffn_g": gain(ks[7], (L, D)),
        "w_in": nrm(ks[8], (L, D, IN_COLS), jnp.float32) * D ** -0.5,
        "q_norm_g": gain(ks[9], (L, HEAD_DIM)),
        "k_norm_g": gain(ks[10], (L, HEAD_DIM)),
        "sinks": nrm(ks[11], (L, N_Q_LOCAL), jnp.float32),
        "grp_norm_a": gain(ks[12], (L, QA)),
        "grp_norm_b": gain(ks[13], (L, QB)),
        "w_out": nrm(ks[14], (L, MIX_WIDTH, D), jnp.float32) * MIX_WIDTH ** -0.5,
        "w_ffn_gate": nrm(ks[15], (N_DENSE, D, F), jnp.float32) * D ** -0.5,
        "w_ffn_up": nrm(ks[16], (N_DENSE, D, F), jnp.float32) * D ** -0.5,
        "w_ffn_down": nrm(ks[17], (N_DENSE, F, D), jnp.float32) * F ** -0.5,
        "w_router": nrm(ks[18], (N_MOE, D, E), jnp.float32) * D ** -0.5,
        "w_exp_gate": nrm(ks[19], (N_MOE, E, D, F), jnp.float32) * D ** -0.5,
        "w_exp_up": nrm(ks[20], (N_MOE, E, D, F), jnp.float32) * D ** -0.5,
        "w_exp_down": nrm(ks[21], (N_MOE, E, F, D), jnp.float32) * F ** -0.5,
    }


def reference(x, c, w_mod, b_mod, pre_mix_g, post_mix_g, pre_ffn_g, post_ffn_g,
              w_in, q_norm_g, k_norm_g, sinks, grp_norm_a, grp_norm_b, w_out,
              w_ffn_gate, w_ffn_up, w_ffn_down,
              w_router, w_exp_gate, w_exp_up, w_exp_down):
    S = x.shape[1]
    rows = S // GRID_W
    row_idx = jnp.repeat(jnp.arange(rows), GRID_W, total_repeat_length=S)
    col_idx = jnp.tile(jnp.arange(GRID_W), rows)
    ang_row = rope_angles(row_idx, HEAD_DIM // 2)
    ang_col = rope_angles(col_idx, HEAD_DIM // 2)
    ang_seq = rope_angles(jnp.arange(S), HEAD_DIM)
    c_act = jax.nn.silu(c)

    for l in range(DEPTH):
        mod = c_act @ w_mod[l] + b_mod[l]
        sh1, sc1, g1, sh2, sc2, g2 = jnp.split(mod, 6, axis=-1)

        h = rms_norm(x, pre_mix_g[l]) * (1.0 + sc1[:, None]) + sh1[:, None]
        y = hybrid_mixer(h, w_in[l], q_norm_g[l], k_norm_g[l], sinks[l],
                         grp_norm_a[l], grp_norm_b[l], w_out[l], ang_row, ang_col, ang_seq)
        x = x + g1[:, None] * rms_norm(y, post_mix_g[l])

        h = rms_norm(x, pre_ffn_g[l]) * (1.0 + sc2[:, None]) + sh2[:, None]
        if l % 2 == 0:
            i = l // 2
            y = swiglu(h, w_ffn_gate[i], w_ffn_up[i], w_ffn_down[i])
        else:
            i = l // 2
            y = moe_swiglu(h, w_router[i], w_exp_gate[i], w_exp_up[i], w_exp_down[i])
        x = x + g2[:, None] * rms_norm(y, post_ffn_g[l])
    return x
```

```python
import functools

import jax
import jax.numpy as jnp
from jax import lax
from jax.experimental import pallas as pl
from jax.experimental.pallas import tpu as pltpu

F32 = jnp.float32
BF16 = jnp.bfloat16

HEAD_DIM = 128
N_Q = 8
N_KV = 2
GQA = N_Q // N_KV
Q_W = N_Q * HEAD_DIM
KV_W = N_KV * HEAD_DIM
IN_COLS = 2 * (Q_W + 2 * KV_W)
WINDOW = 128
GRID_W = 64
ROPE_THETA = 10000.0
N_EXPERTS = 8
EPS = 1e-6
LANES = 128
V7X_VMEM_LIMIT = 56 * 1024 * 1024


def _params(*sem):
    return pltpu.CompilerParams(dimension_semantics=sem, vmem_limit_bytes=V7X_VMEM_LIMIT)


def _resident(shape, index_map):
    return pl.BlockSpec(shape, index_map, pipeline_mode=pl.Buffered(1))


def _rms(v, width):
    return lax.rsqrt(jnp.sum(v * v, axis=-1, keepdims=True) * (1.0 / width) + EPS)


def _silu(v):
    return v * jax.nn.sigmoid(v)


def _mod_body(c_ref, w_ref, b_ref, o_ref):
    ca = _silu(c_ref[...]).astype(BF16)
    o_ref[0] = jnp.dot(ca, w_ref[0].astype(BF16), preferred_element_type=F32) + b_ref[0]


def _modulation(c, w_mod, b_mod):
    n_layers, d, n = w_mod.shape
    b = c.shape[0]
    tn = next(c for c in (1024, 512, 256, LANES) if n % c == 0)
    return pl.pallas_call(
        _mod_body,
        grid=(n_layers, n // tn),
        in_specs=[pl.BlockSpec((b, d), lambda l, j: (0, 0)),
                  pl.BlockSpec((1, d, tn), lambda l, j: (l, 0, j)),
                  pl.BlockSpec((1, 1, tn), lambda l, j: (l, 0, j))],
        out_specs=pl.BlockSpec((1, b, tn), lambda l, j: (l, 0, j)),
        out_shape=jax.ShapeDtypeStruct((n_layers, b, n), F32),
        compiler_params=_params("arbitrary", "arbitrary"),
        name="modulation",
    )(c, w_mod, b_mod.reshape(n_layers, 1, n))


def _rope_tables(seq):
    def angles(pos, dim):
        inv = 1.0 / (ROPE_THETA ** (jnp.arange(0, dim, 2, dtype=F32) / dim))
        return pos.astype(F32)[:, None] * inv[None, :]
    pos = jnp.arange(seq)
    a_row = angles(pos // GRID_W, HEAD_DIM // 2)
    a_col = angles(pos % GRID_W, HEAD_DIM // 2)
    a_seq = angles(pos, HEAD_DIM)
    cos_a = jnp.concatenate([jnp.cos(a_row)] * 2 + [jnp.cos(a_col)] * 2, axis=-1)
    sin_a = jnp.concatenate([-jnp.sin(a_row), jnp.sin(a_row), -jnp.sin(a_col), jnp.sin(a_col)], axis=-1)
    cos_b = jnp.concatenate([jnp.cos(a_seq)] * 2, axis=-1)
    sin_b = jnp.concatenate([-jnp.sin(a_seq), jnp.sin(a_seq)], axis=-1)
    return cos_a, sin_a, cos_b, sin_b


def _head_plan():
    plan = []
    for s in range(IN_COLS // HEAD_DIM):
        if s < 8:
            plan.append(("qa", s))
        elif s < 10:
            plan.append(("ka", s + 8))
        elif s < 12:
            plan.append(("v", s + 8))
        elif s < 20:
            plan.append(("qb", s - 4))
        elif s < 22:
            plan.append(("kb", s))
        else:
            plan.append(("v", s))
    return plan


def _qkv_body(x_ref, g_ref, sh_ref, sc_ref, w_ref, qg_ref, kg_ref,
              ca_ref, sa_ref, cb_ref, sb_ref, o_ref):
    x = x_ref[...]
    d = x.shape[-1]
    xn = x * _rms(x, d) * g_ref[...]
    h = (xn * (1.0 + sc_ref[0]) + sh_ref[0]).astype(BF16)

    tm = x.shape[0]
    lane = lax.broadcasted_iota(jnp.int32, (tm, HEAD_DIM), 1)
    first_quarter = (lane & (HEAD_DIM // 2 - 1)) < (HEAD_DIM // 4)
    ca, sa, cb, sb = ca_ref[...], sa_ref[...], cb_ref[...], sb_ref[...]
    scale = HEAD_DIM ** -0.5

    def norm(a, g):
        return a * _rms(a, HEAD_DIM) * g

    def rope_a(a):
        partner = jnp.where(first_quarter, pltpu.roll(a, 96, 1), pltpu.roll(a, 32, 1))
        return a * ca + partner * sa

    def rope_b(a):
        return a * cb + pltpu.roll(a, 64, 1) * sb

    plan = _head_plan()
    chunk = 2 * HEAD_DIM
    for c in range(IN_COLS // chunk):
        acc = jnp.dot(h, w_ref[:, c * chunk:(c + 1) * chunk], preferred_element_type=F32)
        for half in range(2):
            kind, dst = plan[2 * c + half]
            a = acc[:, half * HEAD_DIM:(half + 1) * HEAD_DIM]
            if kind == "qa":
                a = rope_a(norm(a, qg_ref[...])) * scale
            elif kind == "ka":
                a = rope_a(norm(a, kg_ref[...]))
            elif kind == "qb":
                a = rope_b(a) * scale
            elif kind == "kb":
                a = rope_b(a)
            o_ref[:, dst * HEAD_DIM:(dst + 1) * HEAD_DIM] = a.astype(o_ref.dtype)


def _qkv(xf, mod_l, pre_g, w_in, qg, kg, tables, seq):
    t, d = xf.shape
    tm = min(512, seq)
    per_seq = seq // tm
    row = lambda i: (i, 0)
    tab = lambda i: (i % per_seq, 0)
    vec = pl.BlockSpec((1, d), lambda i: (0, 0))
    hvec = pl.BlockSpec((1, HEAD_DIM), lambda i: (0, 0))
    tspec = pl.BlockSpec((tm, HEAD_DIM), tab)
    return pl.pallas_call(
        _qkv_body,
        grid=(t // tm,),
        in_specs=[pl.BlockSpec((tm, d), row), vec,
                  pl.BlockSpec((1, 1, d), lambda i: ((i // per_seq) * 6 + 0, 0, 0)),
                  pl.BlockSpec((1, 1, d), lambda i: ((i // per_seq) * 6 + 1, 0, 0)),
                  _resident((d, IN_COLS), lambda i: (0, 0)),
                  hvec, hvec, tspec, tspec, tspec, tspec],
        out_specs=pl.BlockSpec((tm, IN_COLS), row),
        out_shape=jax.ShapeDtypeStruct((t, IN_COLS), BF16),
        compiler_params=_params("arbitrary"),
        name="qkv_proj",
    )(xf, pre_g.reshape(1, d), mod_l, mod_l, w_in, qg.reshape(1, HEAD_DIM), kg.reshape(1, HEAD_DIM),
      *tables)


def _attn_body(sinks_ref, qa_ref, qb_ref, kva_ref, kvb_ref, gna_ref, gnb_ref, o_ref, acc_ref,
               *, tq, seq, wk):
    i = pl.program_id(1)
    nt = (((1,), (1,)), ((), ()))

    for kv in range(N_KV):
        k = kva_ref[:, kv * HEAD_DIM:(kv + 1) * HEAD_DIM]
        v = kva_ref[:, KV_W + kv * HEAD_DIM:KV_W + (kv + 1) * HEAD_DIM]
        for g in range(GQA):
            hd = kv * GQA + g
            q = qa_ref[:, hd * HEAD_DIM:(hd + 1) * HEAD_DIM]
            s = lax.dot_general(q, k, nt, preferred_element_type=F32)
            m = jnp.max(s, axis=-1, keepdims=True)
            p = jnp.exp(s - m)
            l = jnp.sum(p, axis=-1, keepdims=True)
            o = jnp.dot(p.astype(BF16), v, preferred_element_type=F32) / l
            acc_ref[:, hd * HEAD_DIM:(hd + 1) * HEAD_DIM] = o

    start = pl.multiple_of(jnp.clip(i * tq - WINDOW, 0, seq - wk), WINDOW)
    qpos = i * tq + lax.broadcasted_iota(jnp.int32, (tq, wk), 0)
    kpos = start + lax.broadcasted_iota(jnp.int32, (tq, wk), 1)
    mask = jnp.abs(kpos - qpos) <= WINDOW
    for kv in range(N_KV):
        k = kvb_ref[pl.ds(start, wk), kv * HEAD_DIM:(kv + 1) * HEAD_DIM]
        v = kvb_ref[pl.ds(start, wk), KV_W + kv * HEAD_DIM:KV_W + (kv + 1) * HEAD_DIM]
        for g in range(GQA):
            hd = kv * GQA + g
            q = qb_ref[:, hd * HEAD_DIM:(hd + 1) * HEAD_DIM]
            sink = sinks_ref[hd]
            s = lax.dot_general(q, k, nt, preferred_element_type=F32)
            s = jnp.where(mask, s, -jnp.inf)
            m = jnp.maximum(jnp.max(s, axis=-1, keepdims=True), sink)
            e = jnp.exp(s - m)
            den = jnp.sum(e, axis=-1, keepdims=True) + jnp.exp(sink - m)
            o = jnp.dot(e.astype(BF16), v, preferred_element_type=F32) / den
            acc_ref[:, Q_W + hd * HEAD_DIM:Q_W + (hd + 1) * HEAD_DIM] = o

    a = acc_ref[:, :Q_W]
    o_ref[:, :Q_W] = (a * _rms(a, Q_W) * gna_ref[...]).astype(o_ref.dtype)
    b = acc_ref[:, Q_W:]
    o_ref[:, Q_W:] = (b * _rms(b, Q_W) * gnb_ref[...]).astype(o_ref.dtype)


def _attention(qkv, sinks, gna, gnb, batch, seq):
    t = qkv.shape[0]
    tq = min(256, seq)
    wk = min(tq + 2 * WINDOW, seq)
    per_seq = seq // tq
    row = lambda b, i: (b * per_seq + i, 0)
    body = functools.partial(_attn_body, tq=tq, seq=seq, wk=wk)
    gvec = pl.BlockSpec((1, Q_W), lambda b, i: (0, 0))
    return pl.pallas_call(
        body,
        grid=(batch, per_seq),
        in_specs=[pl.BlockSpec(memory_space=pltpu.SMEM),
                  pl.BlockSpec((tq, Q_W), lambda b, i: (b * per_seq + i, 0)),
                  pl.BlockSpec((tq, Q_W), lambda b, i: (b * per_seq + i, 1)),
                  pl.BlockSpec((seq, 2 * KV_W), lambda b, i: (b, 4)),
                  pl.BlockSpec((seq, 2 * KV_W), lambda b, i: (b, 5)),
                  gvec, gvec],
        out_specs=pl.BlockSpec((tq, 2 * Q_W), row),
        out_shape=jax.ShapeDtypeStruct((t, 2 * Q_W), BF16),
        scratch_shapes=[pltpu.VMEM((tq, 2 * Q_W), F32)],
        compiler_params=_params("arbitrary", "arbitrary"),
        name="attention",
    )(sinks, qkv, qkv, qkv, qkv, gna.reshape(1, Q_W), gnb.reshape(1, Q_W))


def _outproj_body(*refs, router):
    if router:
        (m_ref, w_ref, x_ref, pg_ref, g1_ref, fg_ref, sh2_ref, sc2_ref, wr_ref,
         x1_ref, h2_ref, rout_ref) = refs
    else:
        m_ref, w_ref, x_ref, pg_ref, g1_ref, fg_ref, sh2_ref, sc2_ref, x1_ref, h2_ref = refs
    y = jnp.dot(m_ref[...], w_ref[...], preferred_element_type=F32)
    d = y.shape[-1]
    x1 = x_ref[...] + g1_ref[0] * (y * _rms(y, d) * pg_ref[...])
    x1_ref[...] = x1
    h2 = (x1 * _rms(x1, d) * fg_ref[...]) * (1.0 + sc2_ref[0]) + sh2_ref[0]
    h2_ref[...] = h2.astype(h2_ref.dtype)
    if router:
        logits = jnp.dot(h2.astype(BF16), wr_ref[...], preferred_element_type=F32)
        lane = lax.broadcasted_iota(jnp.int32, logits.shape, 1).astype(F32)
        lg = jnp.where(lane < N_EXPERTS, logits, -jnp.inf)
        m1 = jnp.max(lg, axis=-1, keepdims=True)
        i1 = jnp.min(jnp.where(lg == m1, lane, float(LANES)), axis=-1, keepdims=True)
        lg2 = jnp.where(lane == i1, -jnp.inf, lg)
        m2 = jnp.max(lg2, axis=-1, keepdims=True)
        i2 = jnp.min(jnp.where(lg2 == m2, lane, float(LANES)), axis=-1, keepdims=True)
        e2 = jnp.exp(m2 - m1)
        den = 1.0 + e2
        rout_ref[...] = jnp.where(lane == 0, i1, jnp.where(lane == 1, i2, jnp.where(
            lane == 2, 1.0 / den, jnp.where(lane == 3, e2 / den, 0.0))))


def _outproj(merged, w_out, xf, post_g, ffn_g, mod_l, seq, w_router=None):
    t, d = xf.shape
    k = merged.shape[1]
    tm = min(256, seq)
    per_seq = seq // tm
    row = lambda i: (i, 0)
    vec = pl.BlockSpec((1, d), lambda i: (0, 0))
    modv = lambda j: pl.BlockSpec((1, 1, d), lambda i: ((i // per_seq) * 6 + j, 0, 0))
    router = w_router is not None
    in_specs = [pl.BlockSpec((tm, k), row), _resident((k, d), lambda i: (0, 0)),
                pl.BlockSpec((tm, d), row), vec, modv(2), vec, modv(3), modv(4)]
    args = [merged, w_out, xf, post_g.reshape(1, d), mod_l, ffn_g.reshape(1, d), mod_l, mod_l]
    out_specs = [pl.BlockSpec((tm, d), row), pl.BlockSpec((tm, d), row)]
    out_shape = [jax.ShapeDtypeStruct((t, d), F32),
                 jax.ShapeDtypeStruct((t, d), F32 if router else BF16)]
    if router:
        wr = jnp.zeros((d, LANES), BF16).at[:, :N_EXPERTS].set(w_router.astype(BF16))
        in_specs.append(_resident((d, LANES), lambda i: (0, 0)))
        args.append(wr)
        out_specs.append(pl.BlockSpec((tm, LANES), row))
        out_shape.append(jax.ShapeDtypeStruct((t, LANES), F32))
    return pl.pallas_call(
        functools.partial(_outproj_body, router=router),
        grid=(t // tm,),
        in_specs=in_specs, out_specs=out_specs, out_shape=out_shape,
        compiler_params=_params("arbitrary"),
        name="out_proj_router" if router else "out_proj",
    )(*args)


def _ffn_body(h_ref, wg_ref, wu_ref, wd_ref, x1_ref, g2_ref, pg_ref, o_ref, acc_ref, *, nf):
    f = pl.program_id(1)

    @pl.when(f == 0)
    def _():
        acc_ref[...] = jnp.zeros_like(acc_ref)

    h = h_ref[...]
    g = jnp.dot(h, wg_ref[...], preferred_element_type=F32)
    u = jnp.dot(h, wu_ref[...], preferred_element_type=F32)
    a = (_silu(g) * u).astype(BF16)
    acc_ref[...] += jnp.dot(a, wd_ref[...], preferred_element_type=F32)

    @pl.when(f == nf - 1)
    def _():
        y = acc_ref[...]
        o_ref[...] = x1_ref[...] + g2_ref[0] * (y * _rms(y, y.shape[-1]) * pg_ref[...])


def _ffn_tile(f_dim):
    for tf in (512, 256, 128):
        if f_dim % tf == 0:
            return tf
    return f_dim


def _dense_ffn(h2, wg, wu, wd, x1, post_g, mod_l, seq):
    t, d = x1.shape
    f_dim = wg.shape[1]
    tm = min(512, seq)
    tf = _ffn_tile(f_dim)
    nf = f_dim // tf
    per_seq = seq // tm
    row = lambda i, f: (i, 0)
    return pl.pallas_call(
        functools.partial(_ffn_body, nf=nf),
        grid=(t // tm, nf),
        in_specs=[pl.BlockSpec((tm, d), row),
                  pl.BlockSpec((d, tf), lambda i, f: (0, f)),
                  pl.BlockSpec((d, tf), lambda i, f: (0, f)),
                  pl.BlockSpec((tf, d), lambda i, f: (f, 0)),
                  pl.BlockSpec((tm, d), row),
                  pl.BlockSpec((1, 1, d), lambda i, f: ((i // per_seq) * 6 + 5, 0, 0)),
                  pl.BlockSpec((1, d), lambda i, f: (0, 0))],
        out_specs=pl.BlockSpec((tm, d), row),
        out_shape=jax.ShapeDtypeStruct((t, d), F32),
        scratch_shapes=[pltpu.VMEM((tm, d), F32)],
        compiler_params=_params("arbitrary", "arbitrary"),
        name="dense_ffn",
    )(h2, wg, wu, wd, x1, mod_l, post_g.reshape(1, d))


def _row_gather(src_hbm, dst_ref, sem, n_rows, row_of):
    def issue(r, carry):
        pltpu.make_async_copy(src_hbm.at[pl.ds(row_of(r), 1)], dst_ref.at[pl.ds(r, 1)], sem).start()
        return carry
    lax.fori_loop(0, n_rows, issue, 0, unroll=8)
    pltpu.make_async_copy(src_hbm.at[pl.ds(0, n_rows)], dst_ref, sem).wait()


def _moe_body(te_ref, nu_ref, rows_ref, h_hbm, wg_ref, wu_ref, wd_ref, y_ref,
              xg_ref, hb_ref, acc_ref, sem, *, tm, nf):
    del te_ref
    i = pl.program_id(0)
    f = pl.program_id(1)
    used = i < nu_ref[0]

    @pl.when(used & (f == 0))
    def _():
        _row_gather(h_hbm, xg_ref, sem, tm, lambda r: rows_ref[0, 0, r])
        hb_ref[...] = xg_ref[...].astype(BF16)
        acc_ref[...] = jnp.zeros_like(acc_ref)

    @pl.when(used)
    def _():
        h = hb_ref[...]
        g = jnp.dot(h, wg_ref[0], preferred_element_type=F32)
        u = jnp.dot(h, wu_ref[0], preferred_element_type=F32)
        a = (_silu(g) * u).astype(BF16)
        acc_ref[...] += jnp.dot(a, wd_ref[0], preferred_element_type=F32)

    @pl.when(used & (f == nf - 1))
    def _():
        y_ref[...] = acc_ref[...]

    @pl.when(jnp.logical_not(used) & (f == nf - 1))
    def _():
        y_ref[...] = jnp.zeros_like(y_ref)


def _combine_body(pos_ref, y_hbm, rout_ref, x1_ref, g2_ref, pg_ref, o_ref, yb_ref, sem, *, tc):
    for k in range(2):
        _row_gather(y_hbm, yb_ref.at[k], sem.at[k], tc, lambda r, k=k: pos_ref[0, 0, 2 * r + k])
    w = rout_ref[...]
    y = w[:, 2:3] * yb_ref[0] + w[:, 3:4] * yb_ref[1]
    o_ref[...] = x1_ref[...] + g2_ref[0] * (y * _rms(y, y.shape[-1]) * pg_ref[...])


def _moe_ffn(h2, rout, wg, wu, wd, x1, post_g, mod_l, seq):
    t, d = x1.shape
    f_dim = wg.shape[2]
    tm = min(512, seq)
    tf = _ffn_tile(f_dim)
    nf = f_dim // tf
    n_pairs = 2 * t
    n_tiles = n_pairs // tm + N_EXPERTS
    n_rows = n_tiles * tm

    expert = rout[:, :2].astype(jnp.int32).reshape(n_pairs)
    onehot = (expert[:, None] == jnp.arange(N_EXPERTS)[None, :]).astype(jnp.int32)
    csum = jnp.cumsum(onehot, axis=0)
    rank = jnp.take_along_axis(csum, expert[:, None], axis=1)[:, 0] - 1
    tiles_per = (csum[-1] + tm - 1) // tm
    tile_end = jnp.cumsum(tiles_per)
    tile_start = tile_end - tiles_per
    n_used = tile_end[-1]
    pos = tile_start[expert] * tm + rank
    rows = jnp.zeros((n_rows,), jnp.int32).at[pos].set(jnp.arange(n_pairs, dtype=jnp.int32) // 2)
    tile_ids = jnp.minimum(jnp.arange(n_tiles, dtype=jnp.int32), n_used - 1)
    tile_expert = jnp.sum((tile_ids[:, None] >= tile_end[None, :]).astype(jnp.int32), axis=1)

    fidx = lambda i, f, te, nu: jnp.where(i < nu[0], f, nf - 1)
    y_sorted = pl.pallas_call(
        functools.partial(_moe_body, tm=tm, nf=nf),
        grid_spec=pltpu.PrefetchScalarGridSpec(
            num_scalar_prefetch=2,
            grid=(n_tiles, nf),
            in_specs=[pl.BlockSpec((1, 1, tm), lambda i, f, te, nu: (i, 0, 0),
                                   memory_space=pltpu.SMEM),
                      pl.BlockSpec(memory_space=pl.ANY),
                      pl.BlockSpec((1, d, tf), lambda i, f, te, nu: (te[i], 0, fidx(i, f, te, nu))),
                      pl.BlockSpec((1, d, tf), lambda i, f, te, nu: (te[i], 0, fidx(i, f, te, nu))),
                      pl.BlockSpec((1, tf, d), lambda i, f, te, nu: (te[i], fidx(i, f, te, nu), 0))],
            out_specs=pl.BlockSpec((tm, d), lambda i, f, te, nu: (i, 0)),
            scratch_shapes=[pltpu.VMEM((tm, d), F32), pltpu.VMEM((tm, d), BF16),
                            pltpu.VMEM((tm, d), F32), pltpu.SemaphoreType.DMA(())]),
        out_shape=jax.ShapeDtypeStruct((n_rows, d), F32),
        compiler_params=_params("arbitrary", "arbitrary"),
        name="expert_ffn",
    )(tile_expert, n_used.reshape(1).astype(jnp.int32), rows.reshape(n_tiles, 1, tm), h2, wg, wu, wd)

    tc = min(256, seq)
    per_seq = seq // tc
    row = lambda i: (i, 0)
    return pl.pallas_call(
        functools.partial(_combine_body, tc=tc),
        grid=(t // tc,),
        in_specs=[pl.BlockSpec((1, 1, 2 * tc), lambda i: (i, 0, 0), memory_space=pltpu.SMEM),
                  pl.BlockSpec(memory_space=pl.ANY),
                  pl.BlockSpec((tc, LANES), row),
                  pl.BlockSpec((tc, d), row),
                  pl.BlockSpec((1, 1, d), lambda i: ((i // per_seq) * 6 + 5, 0, 0)),
                  pl.BlockSpec((1, d), lambda i: (0, 0))],
        out_specs=pl.BlockSpec((tc, d), row),
        out_shape=jax.ShapeDtypeStruct((t, d), F32),
        scratch_shapes=[pltpu.VMEM((2, tc, d), F32), pltpu.SemaphoreType.DMA((2,))],
        compiler_params=_params("arbitrary"),
        name="expert_combine",
    )(pos.reshape(t // tc, 1, 2 * tc), y_sorted, rout, x1, mod_l, post_g.reshape(1, d))


def kernel(x, c, w_mod, b_mod, pre_mix_g, post_mix_g, pre_ffn_g, post_ffn_g, w_in, q_norm_g, k_norm_g, sinks, grp_norm_a, grp_norm_b, w_out, w_ffn_gate, w_ffn_up, w_ffn_down, w_router, w_exp_gate, w_exp_up, w_exp_down):
    batch, seq, d = x.shape
    n_layers = w_mod.shape[0]
    assert seq % 256 == 0 or seq < 256, seq
    assert w_in.shape[2] == IN_COLS and w_out.shape[1] == 2 * Q_W

    mod = _modulation(c, w_mod, b_mod)
    tables = _rope_tables(seq)
    xf = x.reshape(batch * seq, d)
    for l in range(n_layers):
        mod_l = mod[l].reshape(batch * 6, 1, d)
        qkv = _qkv(xf, mod_l, pre_mix_g[l], w_in[l].astype(BF16), q_norm_g[l], k_norm_g[l],
                   tables, seq)
        merged = _attention(qkv, sinks[l], grp_norm_a[l], grp_norm_b[l], batch, seq)
        i = l // 2
        if l % 2 == 0:
            x1, h2 = _outproj(merged, w_out[l].astype(BF16), xf, post_mix_g[l], pre_ffn_g[l],
                              mod_l, seq)
            xf = _dense_ffn(h2, w_ffn_gate[i].astype(BF16), w_ffn_up[i].astype(BF16),
                            w_ffn_down[i].astype(BF16), x1, post_ffn_g[l], mod_l, seq)
        else:
            x1, h2, rout = _outproj(merged, w_out[l].astype(BF16), xf, post_mix_g[l],
                                    pre_ffn_g[l], mod_l, seq, w_router=w_router[i])
            xf = _moe_ffn(h2, rout, w_exp_gate[i].astype(BF16), w_exp_up[i].astype(BF16),
                          w_exp_down[i].astype(BF16), x1, post_ffn_g[l], mod_l, seq)
    return xf.reshape(batch, seq, d)
```

```python
import functools

import jax
import jax.numpy as jnp
from jax import lax
from jax.experimental import pallas as pl
from jax.experimental.pallas import tpu as pltpu

F32 = jnp.float32
BF16 = jnp.bfloat16

HEAD_DIM = 128
N_Q = 8
N_KV = 2
GQA = N_Q // N_KV
Q_W = N_Q * HEAD_DIM
KV_W = N_KV * HEAD_DIM
IN_COLS = 2 * (Q_W + 2 * KV_W)
WINDOW = 128
GRID_W = 64
ROPE_THETA = 10000.0
N_EXPERTS = 8
EPS = 1e-6
LOG2_E = 1.4426950408889634
LANES = 128
V7X_VMEM_LIMIT = 56 * 1024 * 1024


def _params(*sem):
    return pltpu.CompilerParams(dimension_semantics=sem, vmem_limit_bytes=V7X_VMEM_LIMIT)


def _resident(shape, index_map):
    return pl.BlockSpec(shape, index_map, pipeline_mode=pl.Buffered(1))


def _rms(v, width):
    return lax.rsqrt(jnp.sum(v * v, axis=-1, keepdims=True) * (1.0 / width) + EPS)


def _silu(v):
    return v * jax.nn.sigmoid(v)


def _mod_body(c_ref, w_ref, b_ref, o_ref):
    ca = _silu(c_ref[...]).astype(BF16)
    o_ref[0] = jnp.dot(ca, w_ref[0].astype(BF16), preferred_element_type=F32) + b_ref[0]


def _modulation(c, w_mod, b_mod):
    n_layers, d, n = w_mod.shape
    b = c.shape[0]
    tn = next(c for c in (1024, 512, 256, LANES) if n % c == 0)
    return pl.pallas_call(
        _mod_body,
        grid=(n_layers, n // tn),
        in_specs=[pl.BlockSpec((b, d), lambda l, j: (0, 0)),
                  pl.BlockSpec((1, d, tn), lambda l, j: (l, 0, j)),
                  pl.BlockSpec((1, 1, tn), lambda l, j: (l, 0, j))],
        out_specs=pl.BlockSpec((1, b, tn), lambda l, j: (l, 0, j)),
        out_shape=jax.ShapeDtypeStruct((n_layers, b, n), F32),
        compiler_params=_params("arbitrary", "arbitrary"),
        name="modulation",
    )(c, w_mod, b_mod.reshape(n_layers, 1, n))


def _rope_tables(seq):
    def angles(pos, dim):
        inv = 1.0 / (ROPE_THETA ** (jnp.arange(0, dim, 2, dtype=F32) / dim))
        return pos.astype(F32)[:, None] * inv[None, :]
    pos = jnp.arange(seq)
    a_row = angles(pos // GRID_W, HEAD_DIM // 2)
    a_col = angles(pos % GRID_W, HEAD_DIM // 2)
    a_seq = angles(pos, HEAD_DIM)
    cos_a = jnp.concatenate([jnp.cos(a_row)] * 2 + [jnp.cos(a_col)] * 2, axis=-1)
    sin_a = jnp.concatenate([-jnp.sin(a_row), jnp.sin(a_row), -jnp.sin(a_col), jnp.sin(a_col)], axis=-1)
    cos_b = jnp.concatenate([jnp.cos(a_seq)] * 2, axis=-1)
    sin_b = jnp.concatenate([-jnp.sin(a_seq), jnp.sin(a_seq)], axis=-1)
    return cos_a, sin_a, cos_b, sin_b


def _head_plan():
    plan = []
    for s in range(IN_COLS // HEAD_DIM):
        if s < 8:
            plan.append(("qa", s))
        elif s < 10:
            plan.append(("ka", s + 8))
        elif s < 12:
            plan.append(("v", s + 8))
        elif s < 20:
            plan.append(("qb", s - 4))
        elif s < 22:
            plan.append(("kb", s))
        else:
            plan.append(("v", s))
    return plan


def _qkv_body(x_ref, g_ref, sh_ref, sc_ref, w_ref, qg_ref, kg_ref,
              ca_ref, sa_ref, cb_ref, sb_ref, o_ref):
    x = x_ref[...]
    d = x.shape[-1]
    xn = x * _rms(x, d) * g_ref[...]
    h = (xn * (1.0 + sc_ref[0]) + sh_ref[0]).astype(BF16)

    tm = x.shape[0]
    lane = lax.broadcasted_iota(jnp.int32, (tm, HEAD_DIM), 1)
    first_quarter = (lane & (HEAD_DIM // 2 - 1)) < (HEAD_DIM // 4)
    ca, sa, cb, sb = ca_ref[...], sa_ref[...], cb_ref[...], sb_ref[...]
    scale = HEAD_DIM ** -0.5 * LOG2_E

    def norm(a, g):
        return a * _rms(a, HEAD_DIM) * g

    def rope_a(a):
        partner = jnp.where(first_quarter, pltpu.roll(a, 96, 1), pltpu.roll(a, 32, 1))
        return a * ca + partner * sa

    def rope_b(a):
        return a * cb + pltpu.roll(a, 64, 1) * sb

    plan = _head_plan()
    chunk = 2 * HEAD_DIM
    for c in range(IN_COLS // chunk):
        acc = jnp.dot(h, w_ref[:, c * chunk:(c + 1) * chunk], preferred_element_type=F32)
        for half in range(2):
            kind, dst = plan[2 * c + half]
            a = acc[:, half * HEAD_DIM:(half + 1) * HEAD_DIM]
            if kind == "qa":
                a = rope_a(norm(a, qg_ref[...])) * scale
            elif kind == "ka":
                a = rope_a(norm(a, kg_ref[...]))
            elif kind == "qb":
                a = rope_b(a) * scale
            elif kind == "kb":
                a = rope_b(a)
            o_ref[:, dst * HEAD_DIM:(dst + 1) * HEAD_DIM] = a.astype(o_ref.dtype)


def _qkv(xf, mod_l, pre_g, w_in, qg, kg, tables, seq):
    t, d = xf.shape
    tm = min(512, seq)
    per_seq = seq // tm
    row = lambda i: (i, 0)
    tab = lambda i: (i % per_seq, 0)
    vec = pl.BlockSpec((1, d), lambda i: (0, 0))
    hvec = pl.BlockSpec((1, HEAD_DIM), lambda i: (0, 0))
    tspec = pl.BlockSpec((tm, HEAD_DIM), tab)
    return pl.pallas_call(
        _qkv_body,
        grid=(t // tm,),
        in_specs=[pl.BlockSpec((tm, d), row), vec,
                  pl.BlockSpec((1, 1, d), lambda i: ((i // per_seq) * 6 + 0, 0, 0)),
                  pl.BlockSpec((1, 1, d), lambda i: ((i // per_seq) * 6 + 1, 0, 0)),
                  _resident((d, IN_COLS), lambda i: (0, 0)),
                  hvec, hvec, tspec, tspec, tspec, tspec],
        out_specs=pl.BlockSpec((tm, IN_COLS), row),
        out_shape=jax.ShapeDtypeStruct((t, IN_COLS), BF16),
        compiler_params=_params("arbitrary"),
        name="qkv_proj",
    )(xf, pre_g.reshape(1, d), mod_l, mod_l, w_in, qg.reshape(1, HEAD_DIM), kg.reshape(1, HEAD_DIM),
      *tables)


def _transpose_bf16(v):
    return v.astype(F32).T.astype(BF16)


def _reduce_rows(fn, v):
    rows, n = v.shape
    if rows % 128 == 0 and rows > 128:
        v = fn(v.reshape(rows // 128, 128, n), axis=0)
    return fn(v, axis=0, keepdims=True)


def _attn_body(sinks_ref, qa_ref, qb_ref, kva_ref, kvb_ref, gna_ref, gnb_ref, o_ref,
               vt_ref, acc_ref, *, tq, seq, wk):
    i = pl.program_id(1)
    nt = (((1,), (1,)), ((), ()))

    @pl.when(i == 0)
    def _():
        for kv in range(N_KV):
            vt_ref[kv] = _transpose_bf16(kva_ref[:, KV_W + kv * HEAD_DIM:KV_W + (kv + 1) * HEAD_DIM])

    def scores_a(hd):
        k = kva_ref[:, (hd // GQA) * HEAD_DIM:(hd // GQA + 1) * HEAD_DIM]
        q = qa_ref[:, hd * HEAD_DIM:(hd + 1) * HEAD_DIM]
        return lax.dot_general(k, q, nt, preferred_element_type=F32)

    def finish_a(hd, s):
        m = _reduce_rows(jnp.max, s)
        p = jnp.exp2(s - m)
        l = _reduce_rows(jnp.sum, p)
        acc_ref[hd] = jnp.dot(vt_ref[hd // GQA], p.astype(BF16), preferred_element_type=F32) / l

    start = pl.multiple_of(jnp.clip(i * tq - WINDOW, 0, seq - wk), WINDOW)
    kpos = start + lax.broadcasted_iota(jnp.int32, (wk, tq), 0)
    qpos = i * tq + lax.broadcasted_iota(jnp.int32, (wk, tq), 1)
    bias = jnp.where(jnp.abs(kpos - qpos) <= WINDOW, 0.0, -jnp.inf)
    vt_b = [_transpose_bf16(kvb_ref[pl.ds(start, wk), KV_W + kv * HEAD_DIM:KV_W + (kv + 1) * HEAD_DIM])
            for kv in range(N_KV)]

    def scores_b(hd):
        k = kvb_ref[pl.ds(start, wk), (hd // GQA) * HEAD_DIM:(hd // GQA + 1) * HEAD_DIM]
        q = qb_ref[:, hd * HEAD_DIM:(hd + 1) * HEAD_DIM]
        return lax.dot_general(k, q, nt, preferred_element_type=F32) + bias

    def finish_b(hd, s):
        sink = sinks_ref[hd] * LOG2_E
        m = jnp.maximum(_reduce_rows(jnp.max, s), sink)
        e = jnp.exp2(s - m)
        den = _reduce_rows(jnp.sum, e) + jnp.exp2(sink - m)
        acc_ref[N_Q + hd] = jnp.dot(vt_b[hd // GQA], e.astype(BF16), preferred_element_type=F32) / den

    jobs = ([(scores_a, finish_a, hd) for hd in range(N_Q)]
            + [(scores_b, finish_b, hd) for hd in range(N_Q)])
    s_cur = jobs[0][0](jobs[0][2])
    for j, (_, finish, hd) in enumerate(jobs):
        s_next = jobs[j + 1][0](jobs[j + 1][2]) if j + 1 < len(jobs) else None
        finish(hd, s_cur)
        s_cur = s_next

    for grp, gain_ref in enumerate((gna_ref, gnb_ref)):
        ss = jnp.zeros((1, tq), F32)
        for hd in range(N_Q):
            o = acc_ref[grp * N_Q + hd]
            ss = ss + jnp.sum(o * o, axis=0, keepdims=True)
        r = lax.rsqrt(ss * (1.0 / Q_W) + EPS)
        for hd in range(N_Q):
            col = grp * Q_W + hd * HEAD_DIM
            o = (acc_ref[grp * N_Q + hd] * r).T * gain_ref[:, hd * HEAD_DIM:(hd + 1) * HEAD_DIM]
            o_ref[:, col:col + HEAD_DIM] = o.astype(o_ref.dtype)


def _attention(qkv, sinks, gna, gnb, batch, seq):
    t = qkv.shape[0]
    tq = min(256, seq)
    wk = min(tq + 2 * WINDOW, seq)
    per_seq = seq // tq
    row = lambda b, i: (b * per_seq + i, 0)
    body = functools.partial(_attn_body, tq=tq, seq=seq, wk=wk)
    gvec = pl.BlockSpec((1, Q_W), lambda b, i: (0, 0))
    return pl.pallas_call(
        body,
        grid=(batch, per_seq),
        in_specs=[pl.BlockSpec(memory_space=pltpu.SMEM),
                  pl.BlockSpec((tq, Q_W), lambda b, i: (b * per_seq + i, 0)),
                  pl.BlockSpec((tq, Q_W), lambda b, i: (b * per_seq + i, 1)),
                  pl.BlockSpec((seq, 2 * KV_W), lambda b, i: (b, 4)),
                  pl.BlockSpec((seq, 2 * KV_W), lambda b, i: (b, 5)),
                  gvec, gvec],
        out_specs=pl.BlockSpec((tq, 2 * Q_W), row),
        out_shape=jax.ShapeDtypeStruct((t, 2 * Q_W), BF16),
        scratch_shapes=[pltpu.VMEM((N_KV, HEAD_DIM, seq), BF16),
                        pltpu.VMEM((2 * N_Q, HEAD_DIM, tq), F32)],
        compiler_params=_params("arbitrary", "arbitrary"),
        name="attention",
    )(sinks, qkv, qkv, qkv, qkv, gna.reshape(1, Q_W), gnb.reshape(1, Q_W))


def _outproj_body(*refs, router):
    if router:
        (m_ref, w_ref, x_ref, pg_ref, g1_ref, fg_ref, sh2_ref, sc2_ref, wr_ref,
         x1_ref, h2_ref, rout_ref) = refs
    else:
        m_ref, w_ref, x_ref, pg_ref, g1_ref, fg_ref, sh2_ref, sc2_ref, x1_ref, h2_ref = refs
    y = jnp.dot(m_ref[...], w_ref[...], preferred_element_type=F32)
    d = y.shape[-1]
    x1 = x_ref[...] + g1_ref[0] * (y * _rms(y, d) * pg_ref[...])
    x1_ref[...] = x1
    h2 = (x1 * _rms(x1, d) * fg_ref[...]) * (1.0 + sc2_ref[0]) + sh2_ref[0]
    h2_ref[...] = h2.astype(h2_ref.dtype)
    if router:
        logits = jnp.dot(h2.astype(BF16), wr_ref[...], preferred_element_type=F32)
        lane = lax.broadcasted_iota(jnp.int32, logits.shape, 1).astype(F32)
        lg = jnp.where(lane < N_EXPERTS, logits, -jnp.inf)
        m1 = jnp.max(lg, axis=-1, keepdims=True)
        i1 = jnp.min(jnp.where(lg == m1, lane, float(LANES)), axis=-1, keepdims=True)
        lg2 = jnp.where(lane == i1, -jnp.inf, lg)
        m2 = jnp.max(lg2, axis=-1, keepdims=True)
        i2 = jnp.min(jnp.where(lg2 == m2, lane, float(LANES)), axis=-1, keepdims=True)
        e2 = jnp.exp(m2 - m1)
        den = 1.0 + e2
        rout_ref[...] = jnp.where(lane == 0, i1, jnp.where(lane == 1, i2, jnp.where(
            lane == 2, 1.0 / den, jnp.where(lane == 3, e2 / den, 0.0))))


def _outproj(merged, w_out, xf, post_g, ffn_g, mod_l, seq, w_router=None):
    t, d = xf.shape
    k = merged.shape[1]
    tm = min(512, seq)
    per_seq = seq // tm
    row = lambda i: (i, 0)
    vec = pl.BlockSpec((1, d), lambda i: (0, 0))
    modv = lambda j: pl.BlockSpec((1, 1, d), lambda i: ((i // per_seq) * 6 + j, 0, 0))
    router = w_router is not None
    in_specs = [pl.BlockSpec((tm, k), row), _resident((k, d), lambda i: (0, 0)),
                pl.BlockSpec((tm, d), row), vec, modv(2), vec, modv(3), modv(4)]
    args = [merged, w_out, xf, post_g.reshape(1, d), mod_l, ffn_g.reshape(1, d), mod_l, mod_l]
    out_specs = [pl.BlockSpec((tm, d), row), pl.BlockSpec((tm, d), row)]
    out_shape = [jax.ShapeDtypeStruct((t, d), F32),
                 jax.ShapeDtypeStruct((t, d), F32 if router else BF16)]
    if router:
        wr = jnp.zeros((d, LANES), BF16).at[:, :N_EXPERTS].set(w_router.astype(BF16))
        in_specs.append(_resident((d, LANES), lambda i: (0, 0)))
        args.append(wr)
        out_specs.append(pl.BlockSpec((tm, LANES), row))
        out_shape.append(jax.ShapeDtypeStruct((t, LANES), F32))
    return pl.pallas_call(
        functools.partial(_outproj_body, router=router),
        grid=(t // tm,),
        in_specs=in_specs, out_specs=out_specs, out_shape=out_shape,
        compiler_params=_params("arbitrary"),
        name="out_proj_router" if router else "out_proj",
    )(*args)


def _ffn_body(h_ref, wg_ref, wu_ref, wd_ref, x1_ref, g2_ref, pg_ref, o_ref, acc_ref, *, nf):
    f = pl.program_id(1)

    @pl.when(f == 0)
    def _():
        acc_ref[...] = jnp.zeros_like(acc_ref)

    h = h_ref[...]
    g = jnp.dot(h, wg_ref[...], preferred_element_type=F32)
    u = jnp.dot(h, wu_ref[...], preferred_element_type=F32)
    a = (_silu(g) * u).astype(BF16)
    acc_ref[...] += jnp.dot(a, wd_ref[...], preferred_element_type=F32)

    @pl.when(f == nf - 1)
    def _():
        y = acc_ref[...]
        o_ref[...] = x1_ref[...] + g2_ref[0] * (y * _rms(y, y.shape[-1]) * pg_ref[...])


def _ffn_tile(f_dim):
    for tf in (512, 256, 128):
        if f_dim % tf == 0:
            return tf
    return f_dim


def _dense_ffn(h2, wg, wu, wd, x1, post_g, mod_l, seq):
    t, d = x1.shape
    f_dim = wg.shape[1]
    tm = min(512, seq)
    tf = _ffn_tile(f_dim)
    nf = f_dim // tf
    per_seq = seq // tm
    row = lambda i, f: (i, 0)
    return pl.pallas_call(
        functools.partial(_ffn_body, nf=nf),
        grid=(t // tm, nf),
        in_specs=[pl.BlockSpec((tm, d), row),
                  pl.BlockSpec((d, tf), lambda i, f: (0, f)),
                  pl.BlockSpec((d, tf), lambda i, f: (0, f)),
                  pl.BlockSpec((tf, d), lambda i, f: (f, 0)),
                  pl.BlockSpec((tm, d), row),
                  pl.BlockSpec((1, 1, d), lambda i, f: ((i // per_seq) * 6 + 5, 0, 0)),
                  pl.BlockSpec((1, d), lambda i, f: (0, 0))],
        out_specs=pl.BlockSpec((tm, d), row),
        out_shape=jax.ShapeDtypeStruct((t, d), F32),
        scratch_shapes=[pltpu.VMEM((tm, d), F32)],
        compiler_params=_params("arbitrary", "arbitrary"),
        name="dense_ffn",
    )(h2, wg, wu, wd, x1, mod_l, post_g.reshape(1, d))


def _wait_slot(buf_ref, sem):
    pltpu.make_async_copy(buf_ref, buf_ref, sem).wait()


def _moe_body(te_ref, nu_ref, rows_ref, nrows_ref, h_hbm, wg_ref, wu_ref, wd_ref, y_ref,
              xg_ref, hb_ref, acc_ref, sem, *, tm, cs, nf):
    del te_ref
    i = pl.program_id(0)
    f = pl.program_id(1)
    n_used = nu_ref[0]
    used = i < n_used
    slot = lax.rem(i, 2)

    def start_row(idx_ref, chunk, j, dst_slot):
        tok = idx_ref[0, 0, chunk * cs + j]
        pltpu.make_async_copy(h_hbm.at[pl.ds(tok, 1)], xg_ref.at[dst_slot, chunk, pl.ds(j, 1)],
                              sem.at[dst_slot]).start()

    @pl.when((i == 0) & (f == 0))
    def _():
        def chunk_body(c, carry):
            for j in range(cs):
                start_row(rows_ref, c, j, 0)
            return carry
        lax.fori_loop(0, nf, chunk_body, 0)

    @pl.when((i <= n_used) & (f == 0))
    def _():
        _wait_slot(xg_ref.at[slot], sem.at[slot])

    @pl.when(used & (f == 0))
    def _():
        for c in range(nf):
            n = min(cs, tm - c * cs)
            if n > 0:
                hb_ref[c * cs:c * cs + n, :] = xg_ref[slot, c, :n, :].astype(BF16)
        acc_ref[...] = jnp.zeros_like(acc_ref)

    @pl.when(used)
    def _():
        for j in range(cs):
            start_row(nrows_ref, f, j, 1 - slot)
        h = hb_ref[...]
        g = jnp.dot(h, wg_ref[0], preferred_element_type=F32)
        u = jnp.dot(h, wu_ref[0], preferred_element_type=F32)
        a = (_silu(g) * u).astype(BF16)
        acc_ref[...] += jnp.dot(a, wd_ref[0], preferred_element_type=F32)

    @pl.when(used & (f == nf - 1))
    def _():
        y_ref[...] = acc_ref[...]

    @pl.when(jnp.logical_not(used) & (f == nf - 1))
    def _():
        y_ref[...] = jnp.zeros_like(y_ref)


def _combine_body(pos_ref, npos_ref, y_hbm, rout_ref, x1_ref, g2_ref, pg_ref, o_ref, yb_ref, sem,
                  *, tc, n_steps):
    i = pl.program_id(0)
    slot = lax.rem(i, 2)
    d = o_ref.shape[-1]

    def start_rows(idx_ref, dst_slot):
        def body(it, carry):
            for j in range(8):
                for k in range(2):
                    p = idx_ref[0, 0, 2 * (it * 8 + j) + k]
                    pltpu.make_async_copy(y_hbm.at[pl.ds(p, 1)],
                                          yb_ref.at[dst_slot, k, it, pl.ds(j, 1)],
                                          sem.at[dst_slot]).start()
            return carry
        lax.fori_loop(0, tc // 8, body, 0)

    @pl.when(i == 0)
    def _():
        start_rows(pos_ref, 0)

    @pl.when(i + 1 < n_steps)
    def _():
        start_rows(npos_ref, 1 - slot)

    _wait_slot(yb_ref.at[slot], sem.at[slot])
    w = rout_ref[...]
    y = (w[:, 2:3] * yb_ref[slot, 0].reshape(tc, d) + w[:, 3:4] * yb_ref[slot, 1].reshape(tc, d))
    o_ref[...] = x1_ref[...] + g2_ref[0] * (y * _rms(y, d) * pg_ref[...])


def _moe_ffn(h2, rout, wg, wu, wd, x1, post_g, mod_l, seq):
    t, d = x1.shape
    f_dim = wg.shape[2]
    tm = min(512, seq)
    tf = _ffn_tile(f_dim)
    nf = f_dim // tf
    n_pairs = 2 * t
    n_tiles = n_pairs // tm + N_EXPERTS + 1
    n_rows = n_tiles * tm
    cs = -(-tm // (nf * 16)) * 16
    tg = nf * cs

    expert = rout[:, :2].astype(jnp.int32).reshape(n_pairs)
    onehot = (expert[:, None] == jnp.arange(N_EXPERTS)[None, :]).astype(jnp.int32)
    csum = jnp.cumsum(onehot, axis=0)
    rank = jnp.take_along_axis(csum, expert[:, None], axis=1)[:, 0] - 1
    tiles_per = (csum[-1] + tm - 1) // tm
    tile_end = jnp.cumsum(tiles_per)
    tile_start = tile_end - tiles_per
    n_used = tile_end[-1]
    pos = tile_start[expert] * tm + rank
    rows = jnp.zeros((n_rows,), jnp.int32).at[pos].set(jnp.arange(n_pairs, dtype=jnp.int32) // 2)
    tile_ids = jnp.minimum(jnp.arange(n_tiles, dtype=jnp.int32), n_used - 1)
    tile_expert = jnp.sum((tile_ids[:, None] >= tile_end[None, :]).astype(jnp.int32), axis=1)

    rows = jnp.pad(rows.reshape(n_tiles, tm), ((0, 0), (0, tg - tm))).reshape(n_tiles, 1, tg)

    fidx = lambda i, f, te, nu: jnp.where(i < nu[0], f, nf - 1)
    y_sorted = pl.pallas_call(
        functools.partial(_moe_body, tm=tm, cs=cs, nf=nf),
        grid_spec=pltpu.PrefetchScalarGridSpec(
            num_scalar_prefetch=2,
            grid=(n_tiles, nf),
            in_specs=[pl.BlockSpec((1, 1, tg), lambda i, f, te, nu: (i, 0, 0),
                                   memory_space=pltpu.SMEM),
                      pl.BlockSpec((1, 1, tg),
                                   lambda i, f, te, nu: (jnp.minimum(i + 1, n_tiles - 1), 0, 0),
                                   memory_space=pltpu.SMEM),
                      pl.BlockSpec(memory_space=pl.ANY),
                      pl.BlockSpec((1, d, tf), lambda i, f, te, nu: (te[i], 0, fidx(i, f, te, nu))),
                      pl.BlockSpec((1, d, tf), lambda i, f, te, nu: (te[i], 0, fidx(i, f, te, nu))),
                      pl.BlockSpec((1, tf, d), lambda i, f, te, nu: (te[i], fidx(i, f, te, nu), 0))],
            out_specs=pl.BlockSpec((tm, d), lambda i, f, te, nu: (i, 0)),
            scratch_shapes=[pltpu.VMEM((2, nf, cs, d), F32), pltpu.VMEM((tm, d), BF16),
                            pltpu.VMEM((tm, d), F32), pltpu.SemaphoreType.DMA((2,))]),
        out_shape=jax.ShapeDtypeStruct((n_rows, d), F32),
        compiler_params=_params("arbitrary", "arbitrary"),
        name="expert_ffn",
    )(tile_expert, n_used.reshape(1).astype(jnp.int32), rows, rows, h2, wg, wu, wd)

    tc = min(256, seq)
    per_seq = seq // tc
    n_steps = t // tc
    row = lambda i: (i, 0)
    pos = pos.reshape(n_steps, 1, 2 * tc)
    return pl.pallas_call(
        functools.partial(_combine_body, tc=tc, n_steps=n_steps),
        grid=(n_steps,),
        in_specs=[pl.BlockSpec((1, 1, 2 * tc), lambda i: (i, 0, 0), memory_space=pltpu.SMEM),
                  pl.BlockSpec((1, 1, 2 * tc), lambda i: (jnp.minimum(i + 1, n_steps - 1), 0, 0),
                               memory_space=pltpu.SMEM),
                  pl.BlockSpec(memory_space=pl.ANY),
                  pl.BlockSpec((tc, LANES), row),
                  pl.BlockSpec((tc, d), row),
                  pl.BlockSpec((1, 1, d), lambda i: ((i // per_seq) * 6 + 5, 0, 0)),
                  pl.BlockSpec((1, d), lambda i: (0, 0))],
        out_specs=pl.BlockSpec((tc, d), row),
        out_shape=jax.ShapeDtypeStruct((t, d), F32),
        scratch_shapes=[pltpu.VMEM((2, 2, tc // 8, 8, d), F32), pltpu.SemaphoreType.DMA((2,))],
        compiler_params=_params("arbitrary"),
        name="expert_combine",
    )(pos, pos, y_sorted, rout, x1, mod_l, post_g.reshape(1, d))


def kernel(x, c, w_mod, b_mod, pre_mix_g, post_mix_g, pre_ffn_g, post_ffn_g, w_in, q_norm_g, k_norm_g, sinks, grp_norm_a, grp_norm_b, w_out, w_ffn_gate, w_ffn_up, w_ffn_down, w_router, w_exp_gate, w_exp_up, w_exp_down):
    batch, seq, d = x.shape
    n_layers = w_mod.shape[0]
    assert seq % 256 == 0 or seq < 256, seq
    assert w_in.shape[2] == IN_COLS and w_out.shape[1] == 2 * Q_W

    mod = _modulation(c, w_mod, b_mod)
    tables = _rope_tables(seq)
    xf = x.reshape(batch * seq, d)
    for l in range(n_layers):
        mod_l = mod[l].reshape(batch * 6, 1, d)
        qkv = _qkv(xf, mod_l, pre_mix_g[l], w_in[l].astype(BF16), q_norm_g[l], k_norm_g[l],
                   tables, seq)
        merged = _attention(qkv, sinks[l], grp_norm_a[l], grp_norm_b[l], batch, seq)
        i = l // 2
        if l % 2 == 0:
            x1, h2 = _outproj(merged, w_out[l].astype(BF16), xf, post_mix_g[l], pre_ffn_g[l],
                              mod_l, seq)
            xf = _dense_ffn(h2, w_ffn_gate[i].astype(BF16), w_ffn_up[i].astype(BF16),
                            w_ffn_down[i].astype(BF16), x1, post_ffn_g[l], mod_l, seq)
        else:
            x1, h2, rout = _outproj(merged, w_out[l].astype(BF16), xf, post_mix_g[l],
                                    pre_ffn_g[l], mod_l, seq, w_router=w_router[i])
            xf = _moe_ffn(h2, rout, w_exp_gate[i].astype(BF16), w_exp_up[i].astype(BF16),
                          w_exp_down[i].astype(BF16), x1, post_ffn_g[l], mod_l, seq)
    return xf.reshape(batch, seq, d)
```

```python
import functools

import jax
import jax.numpy as jnp
from jax import lax
from jax.experimental import pallas as pl
from jax.experimental.pallas import tpu as pltpu

F32 = jnp.float32
BF16 = jnp.bfloat16

HEAD_DIM = 128
N_Q = 8
N_KV = 2
GQA = N_Q // N_KV
Q_W = N_Q * HEAD_DIM
KV_W = N_KV * HEAD_DIM
IN_COLS = 2 * (Q_W + 2 * KV_W)
WINDOW = 128
GRID_W = 64
ROPE_THETA = 10000.0
N_EXPERTS = 8
EPS = 1e-6
LOG2_E = 1.4426950408889634
HEADS_PER_JOB_A = 2
HEADS_PER_JOB_B = 4
OUTPROJ_ROWS = 256
JOBS_AHEAD = 2
LANES = 128
V7X_VMEM_LIMIT = 56 * 1024 * 1024


def _params(*sem):
    return pltpu.CompilerParams(dimension_semantics=sem, vmem_limit_bytes=V7X_VMEM_LIMIT)


def _resident(shape, index_map):
    return pl.BlockSpec(shape, index_map, pipeline_mode=pl.Buffered(1))


def _rms(v, width):
    return lax.rsqrt(jnp.sum(v * v, axis=-1, keepdims=True) * (1.0 / width) + EPS)


def _silu(v):
    return v * jax.nn.sigmoid(v)


def _mod_body(c_ref, w_ref, b_ref, o_ref):
    ca = _silu(c_ref[...]).astype(BF16)
    o_ref[0] = jnp.dot(ca, w_ref[0].astype(BF16), preferred_element_type=F32) + b_ref[0]


def _modulation(c, w_mod, b_mod):
    n_layers, d, n = w_mod.shape
    b = c.shape[0]
    tn = next(c for c in (1024, 512, 256, LANES) if n % c == 0)
    return pl.pallas_call(
        _mod_body,
        grid=(n_layers, n // tn),
        in_specs=[pl.BlockSpec((b, d), lambda l, j: (0, 0)),
                  pl.BlockSpec((1, d, tn), lambda l, j: (l, 0, j)),
                  pl.BlockSpec((1, 1, tn), lambda l, j: (l, 0, j))],
        out_specs=pl.BlockSpec((1, b, tn), lambda l, j: (l, 0, j)),
        out_shape=jax.ShapeDtypeStruct((n_layers, b, n), F32),
        compiler_params=_params("arbitrary", "arbitrary"),
        name="modulation",
    )(c, w_mod, b_mod.reshape(n_layers, 1, n))


def _rope_tables(seq):
    def angles(pos, dim):
        inv = 1.0 / (ROPE_THETA ** (jnp.arange(0, dim, 2, dtype=F32) / dim))
        return pos.astype(F32)[:, None] * inv[None, :]
    pos = jnp.arange(seq)
    a_row = angles(pos // GRID_W, HEAD_DIM // 2)
    a_col = angles(pos % GRID_W, HEAD_DIM // 2)
    a_seq = angles(pos, HEAD_DIM)
    cos_a = jnp.concatenate([jnp.cos(a_row)] * 2 + [jnp.cos(a_col)] * 2, axis=-1)
    sin_a = jnp.concatenate([-jnp.sin(a_row), jnp.sin(a_row), -jnp.sin(a_col), jnp.sin(a_col)], axis=-1)
    cos_b = jnp.concatenate([jnp.cos(a_seq)] * 2, axis=-1)
    sin_b = jnp.concatenate([-jnp.sin(a_seq), jnp.sin(a_seq)], axis=-1)
    return cos_a, sin_a, cos_b, sin_b


def _head_plan():
    plan = []
    for s in range(IN_COLS // HEAD_DIM):
        if s < 8:
            plan.append(("qa", s))
        elif s < 10:
            plan.append(("ka", s + 8))
        elif s < 12:
            plan.append(("v", s + 8))
        elif s < 20:
            plan.append(("qb", s - 4))
        elif s < 22:
            plan.append(("kb", s))
        else:
            plan.append(("v", s))
    return plan


def _qkv_body(x_ref, g_ref, sh_ref, sc_ref, w_ref, qg_ref, kg_ref,
              ca_ref, sa_ref, cb_ref, sb_ref, o_ref):
    x = x_ref[...]
    d = x.shape[-1]
    xn = x * _rms(x, d) * g_ref[...]
    h = (xn * (1.0 + sc_ref[0]) + sh_ref[0]).astype(BF16)

    tm = x.shape[0]
    lane = lax.broadcasted_iota(jnp.int32, (tm, HEAD_DIM), 1)
    first_quarter = (lane & (HEAD_DIM // 2 - 1)) < (HEAD_DIM // 4)
    ca, sa, cb, sb = ca_ref[...], sa_ref[...], cb_ref[...], sb_ref[...]
    scale = HEAD_DIM ** -0.5 * LOG2_E

    def norm(a, g):
        return a * _rms(a, HEAD_DIM) * g

    def rope_a(a):
        partner = jnp.where(first_quarter, pltpu.roll(a, 96, 1), pltpu.roll(a, 32, 1))
        return a * ca + partner * sa

    def rope_b(a):
        return a * cb + pltpu.roll(a, 64, 1) * sb

    plan = _head_plan()
    chunk = 2 * HEAD_DIM
    for c in range(IN_COLS // chunk):
        acc = jnp.dot(h, w_ref[:, c * chunk:(c + 1) * chunk], preferred_element_type=F32)
        for half in range(2):
            kind, dst = plan[2 * c + half]
            a = acc[:, half * HEAD_DIM:(half + 1) * HEAD_DIM]
            if kind == "qa":
                a = rope_a(norm(a, qg_ref[...])) * scale
            elif kind == "ka":
                a = rope_a(norm(a, kg_ref[...]))
            elif kind == "qb":
                a = rope_b(a) * scale
            elif kind == "kb":
                a = rope_b(a)
            o_ref[:, dst * HEAD_DIM:(dst + 1) * HEAD_DIM] = a.astype(o_ref.dtype)


def _qkv(xf, mod_l, pre_g, w_in, qg, kg, tables, seq):
    t, d = xf.shape
    tm = min(512, seq)
    per_seq = seq // tm
    row = lambda i: (i, 0)
    tab = lambda i: (i % per_seq, 0)
    vec = pl.BlockSpec((1, d), lambda i: (0, 0))
    hvec = pl.BlockSpec((1, HEAD_DIM), lambda i: (0, 0))
    tspec = pl.BlockSpec((tm, HEAD_DIM), tab)
    return pl.pallas_call(
        _qkv_body,
        grid=(t // tm,),
        in_specs=[pl.BlockSpec((tm, d), row), vec,
                  pl.BlockSpec((1, 1, d), lambda i: ((i // per_seq) * 6 + 0, 0, 0)),
                  pl.BlockSpec((1, 1, d), lambda i: ((i // per_seq) * 6 + 1, 0, 0)),
                  _resident((d, IN_COLS), lambda i: (0, 0)),
                  hvec, hvec, tspec, tspec, tspec, tspec],
        out_specs=pl.BlockSpec((tm, IN_COLS), row),
        out_shape=jax.ShapeDtypeStruct((t, IN_COLS), BF16),
        compiler_params=_params("arbitrary"),
        name="qkv_proj",
    )(xf, pre_g.reshape(1, d), mod_l, mod_l, w_in, qg.reshape(1, HEAD_DIM), kg.reshape(1, HEAD_DIM),
      *tables)


ONES_ROWS = 16


def _values_t(v):
    vt = v.astype(F32).T.astype(BF16)
    return jnp.concatenate([vt, jnp.ones((ONES_ROWS, vt.shape[1]), BF16)], axis=0)


def _reduce_rows(fn, v):
    rows, n = v.shape
    if rows % 128 == 0 and rows > 128:
        v = fn(v.reshape(rows // 128, 128, n), axis=0)
    return fn(v, axis=0, keepdims=True)


def _attn_body(sinks_ref, qa_ref, qb_ref, kva_ref, kvb_ref, gna_ref, gnb_ref, o_ref,
               vt_ref, acc_ref, *, tq, seq, wk):
    i = pl.program_id(1)
    nt = (((1,), (1,)), ((), ()))

    @pl.when(i == 0)
    def _():
        for kv in range(N_KV):
            vt_ref[kv] = _values_t(kva_ref[:, KV_W + kv * HEAD_DIM:KV_W + (kv + 1) * HEAD_DIM])

    def stack_q(q_ref, hd0, n):
        qs = [q_ref[:, hd * HEAD_DIM:(hd + 1) * HEAD_DIM] for hd in range(hd0, hd0 + n)]
        return qs[0] if n == 1 else jnp.concatenate(qs, axis=0)

    def scores_a(hd0, n):
        k = kva_ref[:, (hd0 // GQA) * HEAD_DIM:(hd0 // GQA + 1) * HEAD_DIM]
        return lax.dot_general(k, stack_q(qa_ref, hd0, n), nt, preferred_element_type=F32)

    def finish_a(hd0, n, s):
        m = _reduce_rows(jnp.max, s)
        p = jnp.exp2(s - m).astype(BF16)
        o = jnp.dot(vt_ref[hd0 // GQA], p, preferred_element_type=F32)
        o = o[:HEAD_DIM] / o[HEAD_DIM:HEAD_DIM + 1]
        for j in range(n):
            acc_ref[hd0 + j] = o[:, j * tq:(j + 1) * tq]

    start = pl.multiple_of(jnp.clip(i * tq - WINDOW, 0, seq - wk), WINDOW)
    kpos = start + lax.broadcasted_iota(jnp.int32, (wk, tq), 0)
    qpos = i * tq + lax.broadcasted_iota(jnp.int32, (wk, tq), 1)
    bias = jnp.where(jnp.abs(kpos - qpos) <= WINDOW, 0.0, -jnp.inf)
    vt_b = [_values_t(kvb_ref[pl.ds(start, wk), KV_W + kv * HEAD_DIM:KV_W + (kv + 1) * HEAD_DIM])
            for kv in range(N_KV)]

    def scores_b(hd0, n):
        k = kvb_ref[pl.ds(start, wk), (hd0 // GQA) * HEAD_DIM:(hd0 // GQA + 1) * HEAD_DIM]
        s = lax.dot_general(k, stack_q(qb_ref, hd0, n), nt, preferred_element_type=F32)
        return s + (bias if n == 1 else jnp.concatenate([bias] * n, axis=1))

    def finish_b(hd0, n, s):
        sinks = [jnp.full((1, tq), sinks_ref[hd0 + j] * LOG2_E, F32) for j in range(n)]
        sink = sinks[0] if n == 1 else jnp.concatenate(sinks, axis=1)
        m = jnp.maximum(_reduce_rows(jnp.max, s), sink)
        e = jnp.exp2(s - m).astype(BF16)
        o = jnp.dot(vt_b[hd0 // GQA], e, preferred_element_type=F32)
        o = o[:HEAD_DIM] / (o[HEAD_DIM:HEAD_DIM + 1] + jnp.exp2(sink - m))
        for j in range(n):
            acc_ref[N_Q + hd0 + j] = o[:, j * tq:(j + 1) * tq]

    jobs = ([(scores_a, finish_a, hd0, HEADS_PER_JOB_A) for hd0 in range(0, N_Q, HEADS_PER_JOB_A)]
            + [(scores_b, finish_b, hd0, HEADS_PER_JOB_B) for hd0 in range(0, N_Q, HEADS_PER_JOB_B)])
    ahead = min(JOBS_AHEAD, len(jobs))
    pending = [jobs[j][0](jobs[j][2], jobs[j][3]) for j in range(ahead)]
    for j, (_, finish, hd0, n) in enumerate(jobs):
        if j + ahead < len(jobs):
            nxt = jobs[j + ahead]
            pending.append(nxt[0](nxt[2], nxt[3]))
        finish(hd0, n, pending.pop(0))

    for grp, gain_ref in enumerate((gna_ref, gnb_ref)):
        ss = jnp.zeros((1, tq), F32)
        for hd in range(N_Q):
            o = acc_ref[grp * N_Q + hd]
            ss = ss + jnp.sum(o * o, axis=0, keepdims=True)
        r = lax.rsqrt(ss * (1.0 / Q_W) + EPS)
        for hd in range(N_Q):
            col = grp * Q_W + hd * HEAD_DIM
            o = (acc_ref[grp * N_Q + hd] * r).T * gain_ref[:, hd * HEAD_DIM:(hd + 1) * HEAD_DIM]
            o_ref[:, col:col + HEAD_DIM] = o.astype(o_ref.dtype)


def _attention(qkv, sinks, gna, gnb, batch, seq):
    t = qkv.shape[0]
    tq = min(256, seq)
    wk = min(tq + 2 * WINDOW, seq)
    per_seq = seq // tq
    row = lambda b, i: (b * per_seq + i, 0)
    body = functools.partial(_attn_body, tq=tq, seq=seq, wk=wk)
    gvec = pl.BlockSpec((1, Q_W), lambda b, i: (0, 0))
    return pl.pallas_call(
        body,
        grid=(batch, per_seq),
        in_specs=[pl.BlockSpec(memory_space=pltpu.SMEM),
                  pl.BlockSpec((tq, Q_W), lambda b, i: (b * per_seq + i, 0)),
                  pl.BlockSpec((tq, Q_W), lambda b, i: (b * per_seq + i, 1)),
                  pl.BlockSpec((seq, 2 * KV_W), lambda b, i: (b, 4)),
                  pl.BlockSpec((seq, 2 * KV_W), lambda b, i: (b, 5)),
                  gvec, gvec],
        out_specs=pl.BlockSpec((tq, 2 * Q_W), row),
        out_shape=jax.ShapeDtypeStruct((t, 2 * Q_W), BF16),
        scratch_shapes=[pltpu.VMEM((N_KV, HEAD_DIM + ONES_ROWS, seq), BF16),
                        pltpu.VMEM((2 * N_Q, HEAD_DIM, tq), F32)],
        compiler_params=_params("arbitrary", "arbitrary"),
        name="attention",
    )(sinks, qkv, qkv, qkv, qkv, gna.reshape(1, Q_W), gnb.reshape(1, Q_W))


def _outproj_body(*refs, router):
    if router:
        (m_ref, w_ref, x_ref, pg_ref, g1_ref, fg_ref, sh2_ref, sc2_ref, wr_ref,
         x1_ref, h2_ref, rout_ref) = refs
    else:
        m_ref, w_ref, x_ref, pg_ref, g1_ref, fg_ref, sh2_ref, sc2_ref, x1_ref, h2_ref = refs
    tm, d = x_ref.shape
    rows = min(tm, OUTPROJ_ROWS)

    def project(r):
        return jnp.dot(m_ref[r * rows:(r + 1) * rows, :], w_ref[...], preferred_element_type=F32)

    def epilogue(r, y):
        sl = slice(r * rows, (r + 1) * rows)
        x1 = x_ref[sl, :] + g1_ref[0] * (y * _rms(y, d) * pg_ref[...])
        x1_ref[sl, :] = x1
        h2 = (x1 * _rms(x1, d) * fg_ref[...]) * (1.0 + sc2_ref[0]) + sh2_ref[0]
        h2_ref[sl, :] = h2.astype(h2_ref.dtype)
        if router:
            logits = jnp.dot(h2.astype(BF16), wr_ref[...], preferred_element_type=F32)
            lane = lax.broadcasted_iota(jnp.int32, logits.shape, 1).astype(F32)
            lg = jnp.where(lane < N_EXPERTS, logits, -jnp.inf)
            m1 = jnp.max(lg, axis=-1, keepdims=True)
            i1 = jnp.min(jnp.where(lg == m1, lane, float(LANES)), axis=-1, keepdims=True)
            lg2 = jnp.where(lane == i1, -jnp.inf, lg)
            m2 = jnp.max(lg2, axis=-1, keepdims=True)
            i2 = jnp.min(jnp.where(lg2 == m2, lane, float(LANES)), axis=-1, keepdims=True)
            e2 = jnp.exp(m2 - m1)
            den = 1.0 + e2
            rout_ref[sl, :] = jnp.where(lane == 0, i1, jnp.where(lane == 1, i2, jnp.where(
                lane == 2, 1.0 / den, jnp.where(lane == 3, e2 / den, 0.0))))

    y = project(0)
    for r in range(tm // rows):
        y_next = project(r + 1) if r + 1 < tm // rows else None
        epilogue(r, y)
        y = y_next


def _outproj(merged, w_out, xf, post_g, ffn_g, mod_l, seq, w_router=None):
    t, d = xf.shape
    k = merged.shape[1]
    tm = min(512, seq)
    per_seq = seq // tm
    row = lambda i: (i, 0)
    vec = pl.BlockSpec((1, d), lambda i: (0, 0))
    modv = lambda j: pl.BlockSpec((1, 1, d), lambda i: ((i // per_seq) * 6 + j, 0, 0))
    router = w_router is not None
    in_specs = [pl.BlockSpec((tm, k), row), _resident((k, d), lambda i: (0, 0)),
                pl.BlockSpec((tm, d), row), vec, modv(2), vec, modv(3), modv(4)]
    args = [merged, w_out, xf, post_g.reshape(1, d), mod_l, ffn_g.reshape(1, d), mod_l, mod_l]
    out_specs = [pl.BlockSpec((tm, d), row), pl.BlockSpec((tm, d), row)]
    out_shape = [jax.ShapeDtypeStruct((t, d), F32),
                 jax.ShapeDtypeStruct((t, d), F32 if router else BF16)]
    if router:
        wr = jnp.zeros((d, LANES), BF16).at[:, :N_EXPERTS].set(w_router.astype(BF16))
        in_specs.append(_resident((d, LANES), lambda i: (0, 0)))
        args.append(wr)
        out_specs.append(pl.BlockSpec((tm, LANES), row))
        out_shape.append(jax.ShapeDtypeStruct((t, LANES), F32))
    return pl.pallas_call(
        functools.partial(_outproj_body, router=router),
        grid=(t // tm,),
        in_specs=in_specs, out_specs=out_specs, out_shape=out_shape,
        compiler_params=_params("arbitrary"),
        name="out_proj_router" if router else "out_proj",
    )(*args)


def _ffn_body(h_ref, wg_ref, wu_ref, wd_ref, x1_ref, g2_ref, pg_ref, o_ref, acc_ref, *, nf):
    f = pl.program_id(1)

    @pl.when(f == 0)
    def _():
        acc_ref[...] = jnp.zeros_like(acc_ref)

    h = h_ref[...]
    g = jnp.dot(h, wg_ref[...], preferred_element_type=F32)
    u = jnp.dot(h, wu_ref[...], preferred_element_type=F32)
    a = (_silu(g) * u).astype(BF16)
    acc_ref[...] += jnp.dot(a, wd_ref[...], preferred_element_type=F32)

    @pl.when(f == nf - 1)
    def _():
        y = acc_ref[...]
        o_ref[...] = x1_ref[...] + g2_ref[0] * (y * _rms(y, y.shape[-1]) * pg_ref[...])


def _ffn_tile(f_dim):
    for tf in (512, 256, 128):
        if f_dim % tf == 0:
            return tf
    return f_dim


def _dense_ffn(h2, wg, wu, wd, x1, post_g, mod_l, seq):
    t, d = x1.shape
    f_dim = wg.shape[1]
    tm = min(512, seq)
    tf = _ffn_tile(f_dim)
    nf = f_dim // tf
    per_seq = seq // tm
    row = lambda i, f: (i, 0)
    return pl.pallas_call(
        functools.partial(_ffn_body, nf=nf),
        grid=(t // tm, nf),
        in_specs=[pl.BlockSpec((tm, d), row),
                  pl.BlockSpec((d, tf), lambda i, f: (0, f)),
                  pl.BlockSpec((d, tf), lambda i, f: (0, f)),
                  pl.BlockSpec((tf, d), lambda i, f: (f, 0)),
                  pl.BlockSpec((tm, d), row),
                  pl.BlockSpec((1, 1, d), lambda i, f: ((i // per_seq) * 6 + 5, 0, 0)),
                  pl.BlockSpec((1, d), lambda i, f: (0, 0))],
        out_specs=pl.BlockSpec((tm, d), row),
        out_shape=jax.ShapeDtypeStruct((t, d), F32),
        scratch_shapes=[pltpu.VMEM((tm, d), F32)],
        compiler_params=_params("arbitrary", "arbitrary"),
        name="dense_ffn",
    )(h2, wg, wu, wd, x1, mod_l, post_g.reshape(1, d))


def _wait_slot(buf_ref, sem):
    pltpu.make_async_copy(buf_ref, buf_ref, sem).wait()


def _moe_body(te_ref, nu_ref, rows_ref, nrows_ref, h_hbm, wg_ref, wu_ref, wd_ref, y_ref,
              xg_ref, hb_ref, acc_ref, sem, *, tm, cs, nc, nf):
    del te_ref
    i = pl.program_id(0)
    f = pl.program_id(1)
    n_used = nu_ref[0]
    used = i < n_used
    slot = lax.rem(i, 2)

    def start_row(idx_ref, chunk, j, dst_slot):
        tok = idx_ref[0, 0, chunk * cs + j]
        pltpu.make_async_copy(h_hbm.at[pl.ds(tok, 1)], xg_ref.at[dst_slot, chunk, pl.ds(j, 1)],
                              sem.at[dst_slot]).start()

    @pl.when((i == 0) & (f == 0))
    def _():
        def chunk_body(c, carry):
            for j in range(cs):
                start_row(rows_ref, c, j, 0)
            return carry
        lax.fori_loop(0, nc, chunk_body, 0)

    @pl.when((i <= n_used) & (f == 0))
    def _():
        _wait_slot(xg_ref.at[slot], sem.at[slot])

    @pl.when(used & (f == 0))
    def _():
        for c in range(nc):
            n = min(cs, tm - c * cs)
            if n > 0:
                hb_ref[c * cs:c * cs + n, :] = xg_ref[slot, c, :n, :].astype(BF16)
        acc_ref[...] = jnp.zeros_like(acc_ref)

    def ffn_step(issue):
        if issue:
            for j in range(cs):
                start_row(nrows_ref, f, j, 1 - slot)
        h = hb_ref[...]
        g = jnp.dot(h, wg_ref[0], preferred_element_type=F32)
        u = jnp.dot(h, wu_ref[0], preferred_element_type=F32)
        a = (_silu(g) * u).astype(BF16)
        acc_ref[...] += jnp.dot(a, wd_ref[0], preferred_element_type=F32)

    pl.when(used & (f < nc))(functools.partial(ffn_step, True))
    if nc < nf:
        pl.when(used & (f >= nc))(functools.partial(ffn_step, False))

    @pl.when(used & (f == nf - 1))
    def _():
        y_ref[...] = acc_ref[...]

    @pl.when(jnp.logical_not(used) & (f == nf - 1))
    def _():
        y_ref[...] = jnp.zeros_like(y_ref)


def _combine_body(pos_ref, npos_ref, y_hbm, rout_ref, x1_ref, g2_ref, pg_ref, o_ref, yb_ref, sem,
                  *, tc, n_steps):
    i = pl.program_id(0)
    slot = lax.rem(i, 2)
    d = o_ref.shape[-1]

    def start_rows(idx_ref, dst_slot):
        def body(it, carry):
            for j in range(8):
                for k in range(2):
                    p = idx_ref[0, 0, 2 * (it * 8 + j) + k]
                    pltpu.make_async_copy(y_hbm.at[pl.ds(p, 1)],
                                          yb_ref.at[dst_slot, k, it, pl.ds(j, 1)],
                                          sem.at[dst_slot]).start()
            return carry
        lax.fori_loop(0, tc // 8, body, 0)

    @pl.when(i == 0)
    def _():
        start_rows(pos_ref, 0)

    @pl.when(i + 1 < n_steps)
    def _():
        start_rows(npos_ref, 1 - slot)

    _wait_slot(yb_ref.at[slot], sem.at[slot])
    w = rout_ref[...]
    y = (w[:, 2:3] * yb_ref[slot, 0].reshape(tc, d) + w[:, 3:4] * yb_ref[slot, 1].reshape(tc, d))
    o_ref[...] = x1_ref[...] + g2_ref[0] * (y * _rms(y, d) * pg_ref[...])


def _moe_ffn(h2, rout, wg, wu, wd, x1, post_g, mod_l, seq):
    t, d = x1.shape
    f_dim = wg.shape[2]
    tm = min(512, seq)
    tf = _ffn_tile(f_dim)
    nf = f_dim // tf
    n_pairs = 2 * t
    n_tiles = n_pairs // tm + N_EXPERTS + 1
    n_rows = n_tiles * tm
    nc = min(nf, 8)
    cs = -(-tm // (nc * 16)) * 16
    tg = nc * cs

    expert = rout[:, :2].astype(jnp.int32).reshape(n_pairs)
    onehot = (expert[:, None] == jnp.arange(N_EXPERTS)[None, :]).astype(jnp.int32)
    csum = jnp.cumsum(onehot, axis=0)
    rank = jnp.take_along_axis(csum, expert[:, None], axis=1)[:, 0] - 1
    tiles_per = (csum[-1] + tm - 1) // tm
    tile_end = jnp.cumsum(tiles_per)
    tile_start = tile_end - tiles_per
    n_used = tile_end[-1]
    pos = tile_start[expert] * tm + rank
    rows = jnp.zeros((n_rows,), jnp.int32).at[pos].set(jnp.arange(n_pairs, dtype=jnp.int32) // 2)
    tile_ids = jnp.minimum(jnp.arange(n_tiles, dtype=jnp.int32), n_used - 1)
    tile_expert = jnp.sum((tile_ids[:, None] >= tile_end[None, :]).astype(jnp.int32), axis=1)

    rows = jnp.pad(rows.reshape(n_tiles, tm), ((0, 0), (0, tg - tm))).reshape(n_tiles, 1, tg)

    fidx = lambda i, f, te, nu: jnp.where(i < nu[0], f, nf - 1)
    y_sorted = pl.pallas_call(
        functools.partial(_moe_body, tm=tm, cs=cs, nc=nc, nf=nf),
        grid_spec=pltpu.PrefetchScalarGridSpec(
            num_scalar_prefetch=2,
            grid=(n_tiles, nf),
            in_specs=[pl.BlockSpec((1, 1, tg), lambda i, f, te, nu: (i, 0, 0),
                                   memory_space=pltpu.SMEM),
                      pl.BlockSpec((1, 1, tg),
                                   lambda i, f, te, nu: (jnp.minimum(i + 1, n_tiles - 1), 0, 0),
                                   memory_space=pltpu.SMEM),
                      pl.BlockSpec(memory_space=pl.ANY),
                      pl.BlockSpec((1, d, tf), lambda i, f, te, nu: (te[i], 0, fidx(i, f, te, nu))),
                      pl.BlockSpec((1, d, tf), lambda i, f, te, nu: (te[i], 0, fidx(i, f, te, nu))),
                      pl.BlockSpec((1, tf, d), lambda i, f, te, nu: (te[i], fidx(i, f, te, nu), 0))],
            out_specs=pl.BlockSpec((tm, d), lambda i, f, te, nu: (i, 0)),
            scratch_shapes=[pltpu.VMEM((2, nc, cs, d), F32), pltpu.VMEM((tm, d), BF16),
                            pltpu.VMEM((tm, d), F32), pltpu.SemaphoreType.DMA((2,))]),
        out_shape=jax.ShapeDtypeStruct((n_rows, d), F32),
        compiler_params=_params("arbitrary", "arbitrary"),
        name="expert_ffn",
    )(tile_expert, n_used.reshape(1).astype(jnp.int32), rows, rows, h2, wg, wu, wd)

    tc = min(256, seq)
    per_seq = seq // tc
    n_steps = t // tc
    row = lambda i: (i, 0)
    pos = pos.reshape(n_steps, 1, 2 * tc)
    return pl.pallas_call(
        functools.partial(_combine_body, tc=tc, n_steps=n_steps),
        grid=(n_steps,),
        in_specs=[pl.BlockSpec((1, 1, 2 * tc), lambda i: (i, 0, 0), memory_space=pltpu.SMEM),
                  pl.BlockSpec((1, 1, 2 * tc), lambda i: (jnp.minimum(i + 1, n_steps - 1), 0, 0),
                               memory_space=pltpu.SMEM),
                  pl.BlockSpec(memory_space=pl.ANY),
                  pl.BlockSpec((tc, LANES), row),
                  pl.BlockSpec((tc, d), row),
                  pl.BlockSpec((1, 1, d), lambda i: ((i // per_seq) * 6 + 5, 0, 0)),
                  pl.BlockSpec((1, d), lambda i: (0, 0))],
        out_specs=pl.BlockSpec((tc, d), row),
        out_shape=jax.ShapeDtypeStruct((t, d), F32),
        scratch_shapes=[pltpu.VMEM((2, 2, tc // 8, 8, d), F32), pltpu.SemaphoreType.DMA((2,))],
        compiler_params=_params("arbitrary"),
        name="expert_combine",
    )(pos, pos, y_sorted, rout, x1, mod_l, post_g.reshape(1, d))


def kernel(x, c, w_mod, b_mod, pre_mix_g, post_mix_g, pre_ffn_g, post_ffn_g, w_in, q_norm_g, k_norm_g, sinks, grp_norm_a, grp_norm_b, w_out, w_ffn_gate, w_ffn_up, w_ffn_down, w_router, w_exp_gate, w_exp_up, w_exp_down):
    batch, seq, d = x.shape
    n_layers = w_mod.shape[0]
    assert seq % 256 == 0 or seq < 256, seq
    assert w_in.shape[2] == IN_COLS and w_out.shape[1] == 2 * Q_W

    mod = _modulation(c, w_mod, b_mod)
    tables = _rope_tables(seq)
    xf = x.reshape(batch * seq, d)
    for l in range(n_layers):
        mod_l = mod[l].reshape(batch * 6, 1, d)
        qkv = _qkv(xf, mod_l, pre_mix_g[l], w_in[l].astype(BF16), q_norm_g[l], k_norm_g[l],
                   tables, seq)
        merged = _attention(qkv, sinks[l], grp_norm_a[l], grp_norm_b[l], batch, seq)
        i = l // 2
        if l % 2 == 0:
            x1, h2 = _outproj(merged, w_out[l].astype(BF16), xf, post_mix_g[l], pre_ffn_g[l],
                              mod_l, seq)
            xf = _dense_ffn(h2, w_ffn_gate[i].astype(BF16), w_ffn_up[i].astype(BF16),
                            w_ffn_down[i].astype(BF16), x1, post_ffn_g[l], mod_l, seq)
        else:
            x1, h2, rout = _outproj(merged, w_out[l].astype(BF16), xf, post_mix_g[l],
                                    pre_ffn_g[l], mod_l, seq, w_router=w_router[i])
            xf = _moe_ffn(h2, rout, w_exp_gate[i].astype(BF16), w_exp_up[i].astype(BF16),
                          w_exp_down[i].astype(BF16), x1, post_ffn_g[l], mod_l, seq)
    return xf.reshape(batch, seq, d)
```

```python
import functools

import jax
import jax.numpy as jnp
from jax import lax
from jax.experimental import pallas as pl
from jax.experimental.pallas import tpu as pltpu

F32 = jnp.float32
BF16 = jnp.bfloat16

HEAD_DIM = 128
N_Q = 8
N_KV = 2
GQA = N_Q // N_KV
Q_W = N_Q * HEAD_DIM
KV_W = N_KV * HEAD_DIM
IN_COLS = 2 * (Q_W + 2 * KV_W)
WINDOW = 128
GRID_W = 64
ROPE_THETA = 10000.0
N_EXPERTS = 8
EPS = 1e-6
LOG2_E = 1.4426950408889634
HEADS_PER_JOB_A = 2
HEADS_PER_JOB_B = 4
OUTPROJ_ROWS = 256
JOBS_AHEAD = 2
LANES = 128
V7X_VMEM_LIMIT = 56 * 1024 * 1024


def _params(*sem):
    return pltpu.CompilerParams(dimension_semantics=sem, vmem_limit_bytes=V7X_VMEM_LIMIT)


def _resident(shape, index_map):
    return pl.BlockSpec(shape, index_map, pipeline_mode=pl.Buffered(1))


def _rms(v, width):
    return lax.rsqrt(jnp.sum(v * v, axis=-1, keepdims=True) * (1.0 / width) + EPS)


def _silu(v):
    return v * jax.nn.sigmoid(v)


def _mod_body(c_ref, w_ref, b_ref, o_ref):
    ca = _silu(c_ref[...]).astype(BF16)
    o_ref[0] = jnp.dot(ca, w_ref[0].astype(BF16), preferred_element_type=F32) + b_ref[0]


def _modulation(c, w_mod, b_mod):
    n_layers, d, n = w_mod.shape
    b = c.shape[0]
    tn = next(c for c in (1024, 512, 256, LANES) if n % c == 0)
    return pl.pallas_call(
        _mod_body,
        grid=(n_layers, n // tn),
        in_specs=[pl.BlockSpec((b, d), lambda l, j: (0, 0)),
                  pl.BlockSpec((1, d, tn), lambda l, j: (l, 0, j)),
                  pl.BlockSpec((1, 1, tn), lambda l, j: (l, 0, j))],
        out_specs=pl.BlockSpec((1, b, tn), lambda l, j: (l, 0, j)),
        out_shape=jax.ShapeDtypeStruct((n_layers, b, n), F32),
        compiler_params=_params("arbitrary", "arbitrary"),
        name="modulation",
    )(c, w_mod, b_mod.reshape(n_layers, 1, n))


def _rope_tables(seq):
    def angles(pos, dim):
        inv = 1.0 / (ROPE_THETA ** (jnp.arange(0, dim, 2, dtype=F32) / dim))
        return pos.astype(F32)[:, None] * inv[None, :]
    pos = jnp.arange(seq)
    a_row = angles(pos // GRID_W, HEAD_DIM // 2)
    a_col = angles(pos % GRID_W, HEAD_DIM // 2)
    a_seq = angles(pos, HEAD_DIM)
    cos_a = jnp.concatenate([jnp.cos(a_row)] * 2 + [jnp.cos(a_col)] * 2, axis=-1)
    sin_a = jnp.concatenate([-jnp.sin(a_row), jnp.sin(a_row), -jnp.sin(a_col), jnp.sin(a_col)], axis=-1)
    cos_b = jnp.concatenate([jnp.cos(a_seq)] * 2, axis=-1)
    sin_b = jnp.concatenate([-jnp.sin(a_seq), jnp.sin(a_seq)], axis=-1)
    return cos_a, sin_a, cos_b, sin_b


def _head_plan():
    plan = []
    for s in range(IN_COLS // HEAD_DIM):
        if s < 8:
            plan.append(("qa", s))
        elif s < 10:
            plan.append(("ka", s + 8))
        elif s < 12:
            plan.append(("v", s + 8))
        elif s < 20:
            plan.append(("qb", s - 4))
        elif s < 22:
            plan.append(("kb", s))
        else:
            plan.append(("v", s))
    return plan


def _qkv_body(x_ref, g_ref, sh_ref, sc_ref, w_ref, qg_ref, kg_ref,
              ca_ref, sa_ref, cb_ref, sb_ref, o_ref):
    x = x_ref[...]
    d = x.shape[-1]
    xn = x * _rms(x, d) * g_ref[...]
    h = (xn * (1.0 + sc_ref[0]) + sh_ref[0]).astype(BF16)

    tm = x.shape[0]
    lane = lax.broadcasted_iota(jnp.int32, (tm, HEAD_DIM), 1)
    first_quarter = (lane & (HEAD_DIM // 2 - 1)) < (HEAD_DIM // 4)
    ca, sa, cb, sb = ca_ref[...], sa_ref[...], cb_ref[...], sb_ref[...]
    scale = HEAD_DIM ** -0.5 * LOG2_E

    def norm(a, g):
        return a * _rms(a, HEAD_DIM) * g

    def rope_a(a):
        partner = jnp.where(first_quarter, pltpu.roll(a, 96, 1), pltpu.roll(a, 32, 1))
        return a * ca + partner * sa

    def rope_b(a):
        return a * cb + pltpu.roll(a, 64, 1) * sb

    plan = _head_plan()
    chunk = 2 * HEAD_DIM
    for c in range(IN_COLS // chunk):
        acc = jnp.dot(h, w_ref[:, c * chunk:(c + 1) * chunk], preferred_element_type=F32)
        for half in range(2):
            kind, dst = plan[2 * c + half]
            a = acc[:, half * HEAD_DIM:(half + 1) * HEAD_DIM]
            if kind == "qa":
                a = rope_a(norm(a, qg_ref[...])) * scale
            elif kind == "ka":
                a = rope_a(norm(a, kg_ref[...]))
            elif kind == "qb":
                a = rope_b(a) * scale
            elif kind == "kb":
                a = rope_b(a)
            o_ref[:, dst * HEAD_DIM:(dst + 1) * HEAD_DIM] = a.astype(o_ref.dtype)


def _qkv(xf, mod_l, pre_g, w_in, qg, kg, tables, seq):
    t, d = xf.shape
    tm = min(512, seq)
    per_seq = seq // tm
    row = lambda i: (i, 0)
    tab = lambda i: (i % per_seq, 0)
    vec = pl.BlockSpec((1, d), lambda i: (0, 0))
    hvec = pl.BlockSpec((1, HEAD_DIM), lambda i: (0, 0))
    tspec = pl.BlockSpec((tm, HEAD_DIM), tab)
    return pl.pallas_call(
        _qkv_body,
        grid=(t // tm,),
        in_specs=[pl.BlockSpec((tm, d), row), vec,
                  pl.BlockSpec((1, 1, d), lambda i: ((i // per_seq) * 6 + 0, 0, 0)),
                  pl.BlockSpec((1, 1, d), lambda i: ((i // per_seq) * 6 + 1, 0, 0)),
                  _resident((d, IN_COLS), lambda i: (0, 0)),
                  hvec, hvec, tspec, tspec, tspec, tspec],
        out_specs=pl.BlockSpec((tm, IN_COLS), row),
        out_shape=jax.ShapeDtypeStruct((t, IN_COLS), BF16),
        compiler_params=_params("arbitrary"),
        name="qkv_proj",
    )(xf, pre_g.reshape(1, d), mod_l, mod_l, w_in, qg.reshape(1, HEAD_DIM), kg.reshape(1, HEAD_DIM),
      *tables)


ONES_ROWS = 16


def _values_t(v):
    vt = v.astype(F32).T.astype(BF16)
    return jnp.concatenate([vt, jnp.ones((ONES_ROWS, vt.shape[1]), BF16)], axis=0)


def _reduce_rows(fn, v):
    rows, n = v.shape
    if rows % 128 == 0 and rows > 128:
        v = fn(v.reshape(rows // 128, 128, n), axis=0)
    return fn(v, axis=0, keepdims=True)


def _attn_body(sinks_ref, qa_ref, qb_ref, kva_ref, kvb_ref, gna_ref, gnb_ref, o_ref,
               vt_ref, acc_ref, *, tq, seq, wk):
    i = pl.program_id(1)
    nt = (((1,), (1,)), ((), ()))

    @pl.when(i == 0)
    def _():
        for kv in range(N_KV):
            vt_ref[kv] = _values_t(kva_ref[:, KV_W + kv * HEAD_DIM:KV_W + (kv + 1) * HEAD_DIM])

    def stack_q(q_ref, hd0, n):
        qs = [q_ref[:, hd * HEAD_DIM:(hd + 1) * HEAD_DIM] for hd in range(hd0, hd0 + n)]
        return qs[0] if n == 1 else jnp.concatenate(qs, axis=0)

    def scores_a(hd0, n):
        k = kva_ref[:, (hd0 // GQA) * HEAD_DIM:(hd0 // GQA + 1) * HEAD_DIM]
        return lax.dot_general(k, stack_q(qa_ref, hd0, n), nt, preferred_element_type=F32)

    def finish_a(hd0, n, s):
        m = _reduce_rows(jnp.max, s)
        p = jnp.exp2(s - m).astype(BF16)
        o = jnp.dot(vt_ref[hd0 // GQA], p, preferred_element_type=F32)
        o = o[:HEAD_DIM] / o[HEAD_DIM:HEAD_DIM + 1]
        for j in range(n):
            acc_ref[hd0 + j] = o[:, j * tq:(j + 1) * tq]

    start = pl.multiple_of(jnp.clip(i * tq - WINDOW, 0, seq - wk), WINDOW)
    kpos = start + lax.broadcasted_iota(jnp.int32, (wk, tq), 0)
    qpos = i * tq + lax.broadcasted_iota(jnp.int32, (wk, tq), 1)
    bias = jnp.where(jnp.abs(kpos - qpos) <= WINDOW, 0.0, -jnp.inf)
    vt_b = [_values_t(kvb_ref[pl.ds(start, wk), KV_W + kv * HEAD_DIM:KV_W + (kv + 1) * HEAD_DIM])
            for kv in range(N_KV)]

    def scores_b(hd0, n):
        k = kvb_ref[pl.ds(start, wk), (hd0 // GQA) * HEAD_DIM:(hd0 // GQA + 1) * HEAD_DIM]
        s = lax.dot_general(k, stack_q(qb_ref, hd0, n), nt, preferred_element_type=F32)
        return s + (bias if n == 1 else jnp.concatenate([bias] * n, axis=1))

    def finish_b(hd0, n, s):
        sinks = [jnp.full((1, tq), sinks_ref[hd0 + j] * LOG2_E, F32) for j in range(n)]
        sink = sinks[0] if n == 1 else jnp.concatenate(sinks, axis=1)
        m = jnp.maximum(_reduce_rows(jnp.max, s), sink)
        e = jnp.exp2(s - m).astype(BF16)
        o = jnp.dot(vt_b[hd0 // GQA], e, preferred_element_type=F32)
        o = o[:HEAD_DIM] / (o[HEAD_DIM:HEAD_DIM + 1] + jnp.exp2(sink - m))
        for j in range(n):
            acc_ref[N_Q + hd0 + j] = o[:, j * tq:(j + 1) * tq]

    jobs = ([(scores_a, finish_a, hd0, HEADS_PER_JOB_A) for hd0 in range(0, N_Q, HEADS_PER_JOB_A)]
            + [(scores_b, finish_b, hd0, HEADS_PER_JOB_B) for hd0 in range(0, N_Q, HEADS_PER_JOB_B)])
    ahead = min(JOBS_AHEAD, len(jobs))
    pending = [jobs[j][0](jobs[j][2], jobs[j][3]) for j in range(ahead)]
    for j, (_, finish, hd0, n) in enumerate(jobs):
        if j + ahead < len(jobs):
            nxt = jobs[j + ahead]
            pending.append(nxt[0](nxt[2], nxt[3]))
        finish(hd0, n, pending.pop(0))

    for grp, gain_ref in enumerate((gna_ref, gnb_ref)):
        ss = jnp.zeros((1, tq), F32)
        for hd in range(N_Q):
            o = acc_ref[grp * N_Q + hd]
            ss = ss + jnp.sum(o * o, axis=0, keepdims=True)
        r = lax.rsqrt(ss * (1.0 / Q_W) + EPS)
        for hd in range(N_Q):
            col = grp * Q_W + hd * HEAD_DIM
            o = (acc_ref[grp * N_Q + hd] * r).T * gain_ref[:, hd * HEAD_DIM:(hd + 1) * HEAD_DIM]
            o_ref[:, col:col + HEAD_DIM] = o.astype(o_ref.dtype)


def _attention(qkv, sinks, gna, gnb, batch, seq):
    t = qkv.shape[0]
    tq = min(256, seq)
    wk = min(tq + 2 * WINDOW, seq)
    per_seq = seq // tq
    row = lambda b, i: (b * per_seq + i, 0)
    body = functools.partial(_attn_body, tq=tq, seq=seq, wk=wk)
    gvec = pl.BlockSpec((1, Q_W), lambda b, i: (0, 0))
    return pl.pallas_call(
        body,
        grid=(batch, per_seq),
        in_specs=[pl.BlockSpec(memory_space=pltpu.SMEM),
                  pl.BlockSpec((tq, Q_W), lambda b, i: (b * per_seq + i, 0)),
                  pl.BlockSpec((tq, Q_W), lambda b, i: (b * per_seq + i, 1)),
                  pl.BlockSpec((seq, 2 * KV_W), lambda b, i: (b, 4)),
                  pl.BlockSpec((seq, 2 * KV_W), lambda b, i: (b, 5)),
                  gvec, gvec],
        out_specs=pl.BlockSpec((tq, 2 * Q_W), row),
        out_shape=jax.ShapeDtypeStruct((t, 2 * Q_W), BF16),
        scratch_shapes=[pltpu.VMEM((N_KV, HEAD_DIM + ONES_ROWS, seq), BF16),
                        pltpu.VMEM((2 * N_Q, HEAD_DIM, tq), F32)],
        compiler_params=_params("arbitrary", "arbitrary"),
        name="attention",
    )(sinks, qkv, qkv, qkv, qkv, gna.reshape(1, Q_W), gnb.reshape(1, Q_W))


def _outproj_body(*refs, router):
    if router:
        (m_ref, w_ref, x_ref, pg_ref, g1_ref, fg_ref, sh2_ref, sc2_ref, wr_ref,
         x1_ref, h2_ref, rout_ref) = refs
    else:
        m_ref, w_ref, x_ref, pg_ref, g1_ref, fg_ref, sh2_ref, sc2_ref, x1_ref, h2_ref = refs
    tm, d = x_ref.shape
    rows = min(tm, OUTPROJ_ROWS)

    def project(r):
        return jnp.dot(m_ref[r * rows:(r + 1) * rows, :], w_ref[...], preferred_element_type=F32)

    def epilogue(r, y):
        sl = slice(r * rows, (r + 1) * rows)
        x1 = x_ref[sl, :] + g1_ref[0] * (y * _rms(y, d) * pg_ref[...])
        x1_ref[sl, :] = x1
        h2 = (x1 * _rms(x1, d) * fg_ref[...]) * (1.0 + sc2_ref[0]) + sh2_ref[0]
        h2_ref[sl, :] = h2.astype(h2_ref.dtype)
        if router:
            logits = jnp.dot(h2.astype(BF16), wr_ref[...], preferred_element_type=F32)
            lane = lax.broadcasted_iota(jnp.int32, logits.shape, 1).astype(F32)
            lg = jnp.where(lane < N_EXPERTS, logits, -jnp.inf)
            m1 = jnp.max(lg, axis=-1, keepdims=True)
            i1 = jnp.min(jnp.where(lg == m1, lane, float(LANES)), axis=-1, keepdims=True)
            lg2 = jnp.where(lane == i1, -jnp.inf, lg)
            m2 = jnp.max(lg2, axis=-1, keepdims=True)
            i2 = jnp.min(jnp.where(lg2 == m2, lane, float(LANES)), axis=-1, keepdims=True)
            e2 = jnp.exp(m2 - m1)
            den = 1.0 + e2
            rout_ref[sl, :] = jnp.where(lane == 0, i1, jnp.where(lane == 1, i2, jnp.where(
                lane == 2, 1.0 / den, jnp.where(lane == 3, e2 / den, 0.0))))

    y = project(0)
    for r in range(tm // rows):
        y_next = project(r + 1) if r + 1 < tm // rows else None
        epilogue(r, y)
        y = y_next


def _outproj(merged, w_out, xf, post_g, ffn_g, mod_l, seq, w_router=None):
    t, d = xf.shape
    k = merged.shape[1]
    tm = min(512, seq)
    per_seq = seq // tm
    row = lambda i: (i, 0)
    vec = pl.BlockSpec((1, d), lambda i: (0, 0))
    modv = lambda j: pl.BlockSpec((1, 1, d), lambda i: ((i // per_seq) * 6 + j, 0, 0))
    router = w_router is not None
    in_specs = [pl.BlockSpec((tm, k), row), _resident((k, d), lambda i: (0, 0)),
                pl.BlockSpec((tm, d), row), vec, modv(2), vec, modv(3), modv(4)]
    args = [merged, w_out, xf, post_g.reshape(1, d), mod_l, ffn_g.reshape(1, d), mod_l, mod_l]
    out_specs = [pl.BlockSpec((tm, d), row), pl.BlockSpec((tm, d), row)]
    out_shape = [jax.ShapeDtypeStruct((t, d), F32),
                 jax.ShapeDtypeStruct((t, d), F32 if router else BF16)]
    if router:
        wr = jnp.zeros((d, LANES), BF16).at[:, :N_EXPERTS].set(w_router.astype(BF16))
        in_specs.append(_resident((d, LANES), lambda i: (0, 0)))
        args.append(wr)
        out_specs.append(pl.BlockSpec((tm, LANES), row))
        out_shape.append(jax.ShapeDtypeStruct((t, LANES), F32))
    return pl.pallas_call(
        functools.partial(_outproj_body, router=router),
        grid=(t // tm,),
        in_specs=in_specs, out_specs=out_specs, out_shape=out_shape,
        compiler_params=_params("arbitrary"),
        name="out_proj_router" if router else "out_proj",
    )(*args)


def _ffn_body(h_ref, wg_ref, wu_ref, wd_ref, x1_ref, g2_ref, pg_ref, *rest, nf, n_cast):
    cast_in, o_ref = rest[:n_cast], rest[n_cast]
    cast_out, acc_ref = rest[n_cast + 1:2 * n_cast + 1], rest[2 * n_cast + 1]
    f = pl.program_id(1)

    @pl.when(f == 0)
    def _():
        acc_ref[...] = jnp.zeros_like(acc_ref)

    for src, dst in zip(cast_in, cast_out):
        dst[...] = src[...].astype(dst.dtype)

    h = h_ref[...]
    g = jnp.dot(h, wg_ref[...], preferred_element_type=F32)
    u = jnp.dot(h, wu_ref[...], preferred_element_type=F32)
    a = (_silu(g) * u).astype(BF16)
    acc_ref[...] += jnp.dot(a, wd_ref[...], preferred_element_type=F32)

    @pl.when(f == nf - 1)
    def _():
        y = acc_ref[...]
        o_ref[...] = x1_ref[...] + g2_ref[0] * (y * _rms(y, y.shape[-1]) * pg_ref[...])


def _ffn_tile(f_dim):
    for tf in (512, 256, 128):
        if f_dim % tf == 0:
            return tf
    return f_dim


def _cast_spec(arr, n_i, nf):
    rows, cols = arr.shape
    if rows % n_i == 0 and cols % nf == 0 and (rows // n_i) % 16 == 0 and (cols // nf) % LANES == 0:
        return pl.BlockSpec((rows // n_i, cols // nf), lambda i, f: (i, f))
    if rows % (n_i * nf) == 0 and (rows // (n_i * nf)) % 16 == 0:
        return pl.BlockSpec((rows // (n_i * nf), cols), lambda i, f: (i * nf + f, 0))
    return None


def _dense_ffn(h2, wg, wu, wd, x1, post_g, mod_l, seq, to_cast=()):
    t, d = x1.shape
    f_dim = wg.shape[1]
    tm = min(512, seq)
    tf = _ffn_tile(f_dim)
    nf = f_dim // tf
    per_seq = seq // tm
    row = lambda i, f: (i, 0)
    cast_specs = [_cast_spec(a, t // tm, nf) for a in to_cast]
    hosted = [a for a, sp in zip(to_cast, cast_specs) if sp is not None]
    specs = [sp for sp in cast_specs if sp is not None]
    outs = pl.pallas_call(
        functools.partial(_ffn_body, nf=nf, n_cast=len(hosted)),
        grid=(t // tm, nf),
        in_specs=[pl.BlockSpec((tm, d), row),
                  pl.BlockSpec((d, tf), lambda i, f: (0, f)),
                  pl.BlockSpec((d, tf), lambda i, f: (0, f)),
                  pl.BlockSpec((tf, d), lambda i, f: (f, 0)),
                  pl.BlockSpec((tm, d), row),
                  pl.BlockSpec((1, 1, d), lambda i, f: ((i // per_seq) * 6 + 5, 0, 0)),
                  pl.BlockSpec((1, d), lambda i, f: (0, 0))] + specs,
        out_specs=[pl.BlockSpec((tm, d), row)] + specs,
        out_shape=[jax.ShapeDtypeStruct((t, d), F32)]
        + [jax.ShapeDtypeStruct(a.shape, BF16) for a in hosted],
        scratch_shapes=[pltpu.VMEM((tm, d), F32)],
        compiler_params=_params("arbitrary", "arbitrary"),
        name="dense_ffn",
    )(h2, wg, wu, wd, x1, mod_l, post_g.reshape(1, d), *hosted)
    done = iter(outs[1:])
    return outs[0], [next(done) if sp is not None else a.astype(BF16)
                     for a, sp in zip(to_cast, cast_specs)]


def _wait_slot(buf_ref, sem):
    pltpu.make_async_copy(buf_ref, buf_ref, sem).wait()


def _moe_body(te_ref, nu_ref, rows_ref, nrows_ref, h_hbm, wg_ref, wu_ref, wd_ref, y_ref,
              xg_ref, hb_ref, acc_ref, sem, *, tm, cs, nc, nf):
    del te_ref
    i = pl.program_id(0)
    f = pl.program_id(1)
    n_used = nu_ref[0]
    used = i < n_used
    slot = lax.rem(i, 2)

    def start_row(idx_ref, chunk, j, dst_slot):
        tok = idx_ref[0, 0, chunk * cs + j]
        pltpu.make_async_copy(h_hbm.at[pl.ds(tok, 1)], xg_ref.at[dst_slot, chunk, pl.ds(j, 1)],
                              sem.at[dst_slot]).start()

    @pl.when((i == 0) & (f == 0))
    def _():
        def chunk_body(c, carry):
            for j in range(cs):
                start_row(rows_ref, c, j, 0)
            return carry
        lax.fori_loop(0, nc, chunk_body, 0)

    @pl.when((i <= n_used) & (f == 0))
    def _():
        _wait_slot(xg_ref.at[slot], sem.at[slot])

    @pl.when(used & (f == 0))
    def _():
        for c in range(nc):
            n = min(cs, tm - c * cs)
            if n > 0:
                hb_ref[c * cs:c * cs + n, :] = xg_ref[slot, c, :n, :].astype(BF16)
        acc_ref[...] = jnp.zeros_like(acc_ref)

    def ffn_step(issue):
        if issue:
            for j in range(cs):
                start_row(nrows_ref, f, j, 1 - slot)
        h = hb_ref[...]
        g = jnp.dot(h, wg_ref[0], preferred_element_type=F32)
        u = jnp.dot(h, wu_ref[0], preferred_element_type=F32)
        a = (_silu(g) * u).astype(BF16)
        acc_ref[...] += jnp.dot(a, wd_ref[0], preferred_element_type=F32)

    pl.when(used & (f < nc))(functools.partial(ffn_step, True))
    if nc < nf:
        pl.when(used & (f >= nc))(functools.partial(ffn_step, False))

    @pl.when(used & (f == nf - 1))
    def _():
        y_ref[...] = acc_ref[...]

    @pl.when(jnp.logical_not(used) & (f == nf - 1))
    def _():
        y_ref[...] = jnp.zeros_like(y_ref)


def _combine_body(pos_ref, npos_ref, y_hbm, rout_ref, x1_ref, g2_ref, pg_ref, o_ref, yb_ref, sem,
                  *, tc, n_steps):
    i = pl.program_id(0)
    slot = lax.rem(i, 2)
    d = o_ref.shape[-1]

    def start_rows(idx_ref, dst_slot):
        def body(it, carry):
            for j in range(8):
                for k in range(2):
                    p = idx_ref[0, 0, 2 * (it * 8 + j) + k]
                    pltpu.make_async_copy(y_hbm.at[pl.ds(p, 1)],
                                          yb_ref.at[dst_slot, k, it, pl.ds(j, 1)],
                                          sem.at[dst_slot]).start()
            return carry
        lax.fori_loop(0, tc // 8, body, 0)

    @pl.when(i == 0)
    def _():
        start_rows(pos_ref, 0)

    @pl.when(i + 1 < n_steps)
    def _():
        start_rows(npos_ref, 1 - slot)

    _wait_slot(yb_ref.at[slot], sem.at[slot])
    w = rout_ref[...]
    y = (w[:, 2:3] * yb_ref[slot, 0].reshape(tc, d) + w[:, 3:4] * yb_ref[slot, 1].reshape(tc, d))
    o_ref[...] = x1_ref[...] + g2_ref[0] * (y * _rms(y, d) * pg_ref[...])


def _moe_ffn(h2, rout, wg, wu, wd, x1, post_g, mod_l, seq):
    t, d = x1.shape
    f_dim = wg.shape[2]
    tm = min(512, seq)
    tf = _ffn_tile(f_dim)
    nf = f_dim // tf
    n_pairs = 2 * t
    n_tiles = n_pairs // tm + N_EXPERTS + 1
    n_rows = n_tiles * tm
    nc = min(nf, 8)
    cs = -(-tm // (nc * 16)) * 16
    tg = nc * cs

    expert = rout[:, :2].astype(jnp.int32).reshape(n_pairs)
    onehot = (expert[:, None] == jnp.arange(N_EXPERTS)[None, :]).astype(jnp.int32)
    csum = jnp.cumsum(onehot, axis=0)
    rank = jnp.take_along_axis(csum, expert[:, None], axis=1)[:, 0] - 1
    tiles_per = (csum[-1] + tm - 1) // tm
    tile_end = jnp.cumsum(tiles_per)
    tile_start = tile_end - tiles_per
    n_used = tile_end[-1]
    pos = tile_start[expert] * tm + rank
    rows = jnp.zeros((n_rows,), jnp.int32).at[pos].set(jnp.arange(n_pairs, dtype=jnp.int32) // 2)
    tile_ids = jnp.minimum(jnp.arange(n_tiles, dtype=jnp.int32), n_used - 1)
    tile_expert = jnp.sum((tile_ids[:, None] >= tile_end[None, :]).astype(jnp.int32), axis=1)

    rows = jnp.pad(rows.reshape(n_tiles, tm), ((0, 0), (0, tg - tm))).reshape(n_tiles, 1, tg)

    fidx = lambda i, f, te, nu: jnp.where(i < nu[0], f, nf - 1)
    y_sorted = pl.pallas_call(
        functools.partial(_moe_body, tm=tm, cs=cs, nc=nc, nf=nf),
        grid_spec=pltpu.PrefetchScalarGridSpec(
            num_scalar_prefetch=2,
            grid=(n_tiles, nf),
            in_specs=[pl.BlockSpec((1, 1, tg), lambda i, f, te, nu: (i, 0, 0),
                                   memory_space=pltpu.SMEM),
                      pl.BlockSpec((1, 1, tg),
                                   lambda i, f, te, nu: (jnp.minimum(i + 1, n_tiles - 1), 0, 0),
                                   memory_space=pltpu.SMEM),
                      pl.BlockSpec(memory_space=pl.ANY),
                      pl.BlockSpec((1, d, tf), lambda i, f, te, nu: (te[i], 0, fidx(i, f, te, nu))),
                      pl.BlockSpec((1, d, tf), lambda i, f, te, nu: (te[i], 0, fidx(i, f, te, nu))),
                      pl.BlockSpec((1, tf, d), lambda i, f, te, nu: (te[i], fidx(i, f, te, nu), 0))],
            out_specs=pl.BlockSpec((tm, d), lambda i, f, te, nu: (i, 0)),
            scratch_shapes=[pltpu.VMEM((2, nc, cs, d), F32), pltpu.VMEM((tm, d), BF16),
                            pltpu.VMEM((tm, d), F32), pltpu.SemaphoreType.DMA((2,))]),
        out_shape=jax.ShapeDtypeStruct((n_rows, d), F32),
        compiler_params=_params("arbitrary", "arbitrary"),
        name="expert_ffn",
    )(tile_expert, n_used.reshape(1).astype(jnp.int32), rows, rows, h2, wg, wu, wd)

    tc = min(256, seq)
    per_seq = seq // tc
    n_steps = t // tc
    row = lambda i: (i, 0)
    pos = pos.reshape(n_steps, 1, 2 * tc)
    return pl.pallas_call(
        functools.partial(_combine_body, tc=tc, n_steps=n_steps),
        grid=(n_steps,),
        in_specs=[pl.BlockSpec((1, 1, 2 * tc), lambda i: (i, 0, 0), memory_space=pltpu.SMEM),
                  pl.BlockSpec((1, 1, 2 * tc), lambda i: (jnp.minimum(i + 1, n_steps - 1), 0, 0),
                               memory_space=pltpu.SMEM),
                  pl.BlockSpec(memory_space=pl.ANY),
                  pl.BlockSpec((tc, LANES), row),
                  pl.BlockSpec((tc, d), row),
                  pl.BlockSpec((1, 1, d), lambda i: ((i // per_seq) * 6 + 5, 0, 0)),
                  pl.BlockSpec((1, d), lambda i: (0, 0))],
        out_specs=pl.BlockSpec((tc, d), row),
        out_shape=jax.ShapeDtypeStruct((t, d), F32),
        scratch_shapes=[pltpu.VMEM((2, 2, tc // 8, 8, d), F32), pltpu.SemaphoreType.DMA((2,))],
        compiler_params=_params("arbitrary"),
        name="expert_combine",
    )(pos, pos, y_sorted, rout, x1, mod_l, post_g.reshape(1, d))


def kernel(x, c, w_mod, b_mod, pre_mix_g, post_mix_g, pre_ffn_g, post_ffn_g, w_in, q_norm_g, k_norm_g, sinks, grp_norm_a, grp_norm_b, w_out, w_ffn_gate, w_ffn_up, w_ffn_down, w_router, w_exp_gate, w_exp_up, w_exp_down):
    batch, seq, d = x.shape
    n_layers = w_mod.shape[0]
    assert seq % 256 == 0 or seq < 256, seq
    assert w_in.shape[2] == IN_COLS and w_out.shape[1] == 2 * Q_W

    mod = _modulation(c, w_mod, b_mod)
    tables = _rope_tables(seq)
    xf = x.reshape(batch * seq, d)
    expert_w = None
    for l in range(n_layers):
        mod_l = mod[l].reshape(batch * 6, 1, d)
        qkv = _qkv(xf, mod_l, pre_mix_g[l], w_in[l].astype(BF16), q_norm_g[l], k_norm_g[l],
                   tables, seq)
        merged = _attention(qkv, sinks[l], grp_norm_a[l], grp_norm_b[l], batch, seq)
        i = l // 2
        if l % 2 == 0:
            x1, h2 = _outproj(merged, w_out[l].astype(BF16), xf, post_mix_g[l], pre_ffn_g[l],
                              mod_l, seq)
            to_cast = ()
            if l + 1 < n_layers:
                to_cast = [w.reshape(-1, w.shape[-1])
                           for w in (w_exp_gate[i], w_exp_up[i], w_exp_down[i])]
            xf, cast = _dense_ffn(h2, w_ffn_gate[i].astype(BF16), w_ffn_up[i].astype(BF16),
                                  w_ffn_down[i].astype(BF16), x1, post_ffn_g[l], mod_l, seq,
                                  to_cast=to_cast)
            if cast:
                expert_w = [c2.reshape(w.shape[1:]) for c2, w in
                            zip(cast, (w_exp_gate, w_exp_up, w_exp_down))]
        else:
            x1, h2, rout = _outproj(merged, w_out[l].astype(BF16), xf, post_mix_g[l],
                                    pre_ffn_g[l], mod_l, seq, w_router=w_router[i])
            if expert_w is None:
                expert_w = [w[i].astype(BF16) for w in (w_exp_gate, w_exp_up, w_exp_down)]
            xf = _moe_ffn(h2, rout, *expert_w, x1, post_ffn_g[l], mod_l, seq)
            expert_w = None
    return xf.reshape(batch, seq, d)
```

```python
import functools

import jax
import jax.numpy as jnp
from jax import lax
from jax.experimental import pallas as pl
from jax.experimental.pallas import tpu as pltpu

F32 = jnp.float32
BF16 = jnp.bfloat16

HEAD_DIM = 128
N_Q = 8
N_KV = 2
GQA = N_Q // N_KV
Q_W = N_Q * HEAD_DIM
KV_W = N_KV * HEAD_DIM
IN_COLS = 2 * (Q_W + 2 * KV_W)
WINDOW = 128
GRID_W = 64
ROPE_THETA = 10000.0
N_EXPERTS = 8
EPS = 1e-6
LOG2_E = 1.4426950408889634
HEADS_PER_JOB_A = 2
HEADS_PER_JOB_B = 4
OUTPROJ_ROWS = 256
JOBS_AHEAD = 2
LANES = 128
V7X_VMEM_LIMIT = 56 * 1024 * 1024


def _params(*sem):
    return pltpu.CompilerParams(dimension_semantics=sem, vmem_limit_bytes=V7X_VMEM_LIMIT)


def _resident(shape, index_map):
    return pl.BlockSpec(shape, index_map, pipeline_mode=pl.Buffered(1))


def _rms(v, width):
    return lax.rsqrt(jnp.sum(v * v, axis=-1, keepdims=True) * (1.0 / width) + EPS)


def _silu(v):
    return v * jax.nn.sigmoid(v)


def _mod_body(c_ref, w_ref, b_ref, o_ref):
    ca = _silu(c_ref[...]).astype(BF16)
    o_ref[0] = jnp.dot(ca, w_ref[0].astype(BF16), preferred_element_type=F32) + b_ref[0]


def _modulation(c, w_mod, b_mod):
    n_layers, d, n = w_mod.shape
    b = c.shape[0]
    tn = next(c for c in (1024, 512, 256, LANES) if n % c == 0)
    return pl.pallas_call(
        _mod_body,
        grid=(n_layers, n // tn),
        in_specs=[pl.BlockSpec((b, d), lambda l, j: (0, 0)),
                  pl.BlockSpec((1, d, tn), lambda l, j: (l, 0, j)),
                  pl.BlockSpec((1, 1, tn), lambda l, j: (l, 0, j))],
        out_specs=pl.BlockSpec((1, b, tn), lambda l, j: (l, 0, j)),
        out_shape=jax.ShapeDtypeStruct((n_layers, b, n), F32),
        compiler_params=_params("arbitrary", "arbitrary"),
        name="modulation",
    )(c, w_mod, b_mod.reshape(n_layers, 1, n))


def _rope_tables(seq):
    def angles(pos, dim):
        inv = 1.0 / (ROPE_THETA ** (jnp.arange(0, dim, 2, dtype=F32) / dim))
        return pos.astype(F32)[:, None] * inv[None, :]
    pos = jnp.arange(seq)
    a_row = angles(pos // GRID_W, HEAD_DIM // 2)
    a_col = angles(pos % GRID_W, HEAD_DIM // 2)
    a_seq = angles(pos, HEAD_DIM)
    cos_a = jnp.concatenate([jnp.cos(a_row)] * 2 + [jnp.cos(a_col)] * 2, axis=-1)
    sin_a = jnp.concatenate([-jnp.sin(a_row), jnp.sin(a_row), -jnp.sin(a_col), jnp.sin(a_col)], axis=-1)
    cos_b = jnp.concatenate([jnp.cos(a_seq)] * 2, axis=-1)
    sin_b = jnp.concatenate([-jnp.sin(a_seq), jnp.sin(a_seq)], axis=-1)
    return cos_a, sin_a, cos_b, sin_b


def _head_plan():
    plan = []
    for s in range(IN_COLS // HEAD_DIM):
        if s < 8:
            plan.append(("qa", s))
        elif s < 10:
            plan.append(("ka", s + 8))
        elif s < 12:
            plan.append(("v", s + 8))
        elif s < 20:
            plan.append(("qb", s - 4))
        elif s < 22:
            plan.append(("kb", s))
        else:
            plan.append(("v", s))
    return plan


def _qkv_body(x_ref, g_ref, sh_ref, sc_ref, w_ref, qg_ref, kg_ref,
              ca_ref, sa_ref, cb_ref, sb_ref, o_ref):
    x = x_ref[...]
    d = x.shape[-1]
    xn = x * _rms(x, d) * g_ref[...]
    h = (xn * (1.0 + sc_ref[0]) + sh_ref[0]).astype(BF16)

    tm = x.shape[0]
    lane = lax.broadcasted_iota(jnp.int32, (tm, HEAD_DIM), 1)
    first_quarter = (lane & (HEAD_DIM // 2 - 1)) < (HEAD_DIM // 4)
    ca, sa, cb, sb = ca_ref[...], sa_ref[...], cb_ref[...], sb_ref[...]
    scale = HEAD_DIM ** -0.5 * LOG2_E

    def norm(a, g):
        return a * _rms(a, HEAD_DIM) * g

    def rope_a(a):
        partner = jnp.where(first_quarter, pltpu.roll(a, 96, 1), pltpu.roll(a, 32, 1))
        return a * ca + partner * sa

    def rope_b(a):
        return a * cb + pltpu.roll(a, 64, 1) * sb

    plan = _head_plan()
    chunk = 2 * HEAD_DIM
    for c in range(IN_COLS // chunk):
        acc = jnp.dot(h, w_ref[:, c * chunk:(c + 1) * chunk], preferred_element_type=F32)
        for half in range(2):
            kind, dst = plan[2 * c + half]
            a = acc[:, half * HEAD_DIM:(half + 1) * HEAD_DIM]
            if kind == "qa":
                a = rope_a(norm(a, qg_ref[...])) * scale
            elif kind == "ka":
                a = rope_a(norm(a, kg_ref[...]))
            elif kind == "qb":
                a = rope_b(a) * scale
            elif kind == "kb":
                a = rope_b(a)
            o_ref[:, dst * HEAD_DIM:(dst + 1) * HEAD_DIM] = a.astype(o_ref.dtype)


def _qkv(xf, mod_l, pre_g, w_in, qg, kg, tables, seq):
    t, d = xf.shape
    tm = min(512, seq)
    per_seq = seq // tm
    row = lambda i: (i, 0)
    tab = lambda i: (i % per_seq, 0)
    vec = pl.BlockSpec((1, d), lambda i: (0, 0))
    hvec = pl.BlockSpec((1, HEAD_DIM), lambda i: (0, 0))
    tspec = pl.BlockSpec((tm, HEAD_DIM), tab)
    return pl.pallas_call(
        _qkv_body,
        grid=(t // tm,),
        in_specs=[pl.BlockSpec((tm, d), row), vec,
                  pl.BlockSpec((1, 1, d), lambda i: ((i // per_seq) * 6 + 0, 0, 0)),
                  pl.BlockSpec((1, 1, d), lambda i: ((i // per_seq) * 6 + 1, 0, 0)),
                  _resident((d, IN_COLS), lambda i: (0, 0)),
                  hvec, hvec, tspec, tspec, tspec, tspec],
        out_specs=pl.BlockSpec((tm, IN_COLS), row),
        out_shape=jax.ShapeDtypeStruct((t, IN_COLS), BF16),
        compiler_params=_params("arbitrary"),
        name="qkv_proj",
    )(xf, pre_g.reshape(1, d), mod_l, mod_l, w_in, qg.reshape(1, HEAD_DIM), kg.reshape(1, HEAD_DIM),
      *tables)


ONES_ROWS = 16


def _values_t(v):
    vt = v.astype(F32).T.astype(BF16)
    return jnp.concatenate([vt, jnp.ones((ONES_ROWS, vt.shape[1]), BF16)], axis=0)


def _reduce_rows(fn, v):
    rows, n = v.shape
    if rows % 128 == 0 and rows > 128:
        v = fn(v.reshape(rows // 128, 128, n), axis=0)
    return fn(v, axis=0, keepdims=True)


def _attn_body(sinks_ref, qa_ref, qb_ref, kva_ref, kvb_ref, gna_ref, gnb_ref, o_ref,
               vt_ref, acc_ref, *, tq, seq, wk):
    i = pl.program_id(1)
    nt = (((1,), (1,)), ((), ()))

    @pl.when(i == 0)
    def _():
        for kv in range(N_KV):
            vt_ref[kv] = _values_t(kva_ref[:, KV_W + kv * HEAD_DIM:KV_W + (kv + 1) * HEAD_DIM])

    def stack_q(q_ref, hd0, n):
        qs = [q_ref[:, hd * HEAD_DIM:(hd + 1) * HEAD_DIM] for hd in range(hd0, hd0 + n)]
        return qs[0] if n == 1 else jnp.concatenate(qs, axis=0)

    def scores_a(hd0, n):
        k = kva_ref[:, (hd0 // GQA) * HEAD_DIM:(hd0 // GQA + 1) * HEAD_DIM]
        return lax.dot_general(k, stack_q(qa_ref, hd0, n), nt, preferred_element_type=F32)

    def finish_a(hd0, n, s):
        m = _reduce_rows(jnp.max, s)
        p = jnp.exp2(s - m).astype(BF16)
        o = jnp.dot(vt_ref[hd0 // GQA], p, preferred_element_type=F32)
        o = o[:HEAD_DIM] / o[HEAD_DIM:HEAD_DIM + 1]
        for j in range(n):
            acc_ref[hd0 + j] = o[:, j * tq:(j + 1) * tq]

    start = pl.multiple_of(jnp.clip(i * tq - WINDOW, 0, seq - wk), WINDOW)
    kpos = start + lax.broadcasted_iota(jnp.int32, (wk, tq), 0)
    qpos = i * tq + lax.broadcasted_iota(jnp.int32, (wk, tq), 1)
    bias = jnp.where(jnp.abs(kpos - qpos) <= WINDOW, 0.0, -jnp.inf)
    vt_b = [_values_t(kvb_ref[pl.ds(start, wk), KV_W + kv * HEAD_DIM:KV_W + (kv + 1) * HEAD_DIM])
            for kv in range(N_KV)]

    def scores_b(hd0, n):
        k = kvb_ref[pl.ds(start, wk), (hd0 // GQA) * HEAD_DIM:(hd0 // GQA + 1) * HEAD_DIM]
        s = lax.dot_general(k, stack_q(qb_ref, hd0, n), nt, preferred_element_type=F32)
        return s + (bias if n == 1 else jnp.concatenate([bias] * n, axis=1))

    def finish_b(hd0, n, s):
        sinks = [jnp.full((1, tq), sinks_ref[hd0 + j] * LOG2_E, F32) for j in range(n)]
        sink = sinks[0] if n == 1 else jnp.concatenate(sinks, axis=1)
        m = jnp.maximum(_reduce_rows(jnp.max, s), sink)
        e = jnp.exp2(s - m).astype(BF16)
        o = jnp.dot(vt_b[hd0 // GQA], e, preferred_element_type=F32)
        o = o[:HEAD_DIM] / (o[HEAD_DIM:HEAD_DIM + 1] + jnp.exp2(sink - m))
        for j in range(n):
            acc_ref[N_Q + hd0 + j] = o[:, j * tq:(j + 1) * tq]

    jobs = ([(scores_a, finish_a, hd0, HEADS_PER_JOB_A) for hd0 in range(0, N_Q, HEADS_PER_JOB_A)]
            + [(scores_b, finish_b, hd0, HEADS_PER_JOB_B) for hd0 in range(0, N_Q, HEADS_PER_JOB_B)])
    ahead = min(JOBS_AHEAD, len(jobs))
    pending = [jobs[j][0](jobs[j][2], jobs[j][3]) for j in range(ahead)]
    for j, (_, finish, hd0, n) in enumerate(jobs):
        if j + ahead < len(jobs):
            nxt = jobs[j + ahead]
            pending.append(nxt[0](nxt[2], nxt[3]))
        finish(hd0, n, pending.pop(0))

    for grp, gain_ref in enumerate((gna_ref, gnb_ref)):
        ss = jnp.zeros((1, tq), F32)
        for hd in range(N_Q):
            o = acc_ref[grp * N_Q + hd]
            ss = ss + jnp.sum(o * o, axis=0, keepdims=True)
        r = lax.rsqrt(ss * (1.0 / Q_W) + EPS)
        for hd in range(N_Q):
            col = grp * Q_W + hd * HEAD_DIM
            o = (acc_ref[grp * N_Q + hd] * r).T * gain_ref[:, hd * HEAD_DIM:(hd + 1) * HEAD_DIM]
            o_ref[:, col:col + HEAD_DIM] = o.astype(o_ref.dtype)


def _attention(qkv, sinks, gna, gnb, batch, seq):
    t = qkv.shape[0]
    tq = min(256, seq)
    wk = min(tq + 2 * WINDOW, seq)
    per_seq = seq // tq
    row = lambda b, i: (b * per_seq + i, 0)
    body = functools.partial(_attn_body, tq=tq, seq=seq, wk=wk)
    gvec = pl.BlockSpec((1, Q_W), lambda b, i: (0, 0))
    return pl.pallas_call(
        body,
        grid=(batch, per_seq),
        in_specs=[pl.BlockSpec(memory_space=pltpu.SMEM),
                  pl.BlockSpec((tq, Q_W), lambda b, i: (b * per_seq + i, 0)),
                  pl.BlockSpec((tq, Q_W), lambda b, i: (b * per_seq + i, 1)),
                  pl.BlockSpec((seq, 2 * KV_W), lambda b, i: (b, 4)),
                  pl.BlockSpec((seq, 2 * KV_W), lambda b, i: (b, 5)),
                  gvec, gvec],
        out_specs=pl.BlockSpec((tq, 2 * Q_W), row),
        out_shape=jax.ShapeDtypeStruct((t, 2 * Q_W), BF16),
        scratch_shapes=[pltpu.VMEM((N_KV, HEAD_DIM + ONES_ROWS, seq), BF16),
                        pltpu.VMEM((2 * N_Q, HEAD_DIM, tq), F32)],
        compiler_params=_params("arbitrary", "arbitrary"),
        name="attention",
    )(sinks, qkv, qkv, qkv, qkv, gna.reshape(1, Q_W), gnb.reshape(1, Q_W))


def _outproj_body(*refs, router):
    if router:
        (m_ref, w_ref, x_ref, pg_ref, g1_ref, fg_ref, sh2_ref, sc2_ref, wr_ref,
         x1_ref, h2_ref, rout_ref, ridx_ref) = refs
    else:
        m_ref, w_ref, x_ref, pg_ref, g1_ref, fg_ref, sh2_ref, sc2_ref, x1_ref, h2_ref = refs
    tm, d = x_ref.shape
    rows = min(tm, OUTPROJ_ROWS)

    def project(r):
        return jnp.dot(m_ref[r * rows:(r + 1) * rows, :], w_ref[...], preferred_element_type=F32)

    def epilogue(r, y):
        sl = slice(r * rows, (r + 1) * rows)
        x1 = x_ref[sl, :] + g1_ref[0] * (y * _rms(y, d) * pg_ref[...])
        x1_ref[sl, :] = x1
        h2 = (x1 * _rms(x1, d) * fg_ref[...]) * (1.0 + sc2_ref[0]) + sh2_ref[0]
        h2_ref[sl, :] = h2.astype(h2_ref.dtype)
        if router:
            logits = jnp.dot(h2.astype(BF16), wr_ref[...], preferred_element_type=F32)
            lane = lax.broadcasted_iota(jnp.int32, logits.shape, 1).astype(F32)
            lg = jnp.where(lane < N_EXPERTS, logits, -jnp.inf)
            m1 = jnp.max(lg, axis=-1, keepdims=True)
            i1 = jnp.min(jnp.where(lg == m1, lane, float(LANES)), axis=-1, keepdims=True)
            lg2 = jnp.where(lane == i1, -jnp.inf, lg)
            m2 = jnp.max(lg2, axis=-1, keepdims=True)
            i2 = jnp.min(jnp.where(lg2 == m2, lane, float(LANES)), axis=-1, keepdims=True)
            e2 = jnp.exp(m2 - m1)
            den = 1.0 + e2
            rout = jnp.where(lane == 0, i1, jnp.where(lane == 1, i2, jnp.where(
                lane == 2, 1.0 / den, jnp.where(lane == 3, e2 / den, 0.0))))
            rout_ref[sl, :] = rout
            ridx_ref[:, sl] = rout.T[:8, :]

    y = project(0)
    for r in range(tm // rows):
        y_next = project(r + 1) if r + 1 < tm // rows else None
        epilogue(r, y)
        y = y_next


def _outproj(merged, w_out, xf, post_g, ffn_g, mod_l, seq, w_router=None):
    t, d = xf.shape
    k = merged.shape[1]
    tm = min(512, seq)
    per_seq = seq // tm
    row = lambda i: (i, 0)
    vec = pl.BlockSpec((1, d), lambda i: (0, 0))
    modv = lambda j: pl.BlockSpec((1, 1, d), lambda i: ((i // per_seq) * 6 + j, 0, 0))
    router = w_router is not None
    in_specs = [pl.BlockSpec((tm, k), row), _resident((k, d), lambda i: (0, 0)),
                pl.BlockSpec((tm, d), row), vec, modv(2), vec, modv(3), modv(4)]
    args = [merged, w_out, xf, post_g.reshape(1, d), mod_l, ffn_g.reshape(1, d), mod_l, mod_l]
    out_specs = [pl.BlockSpec((tm, d), row), pl.BlockSpec((tm, d), row)]
    out_shape = [jax.ShapeDtypeStruct((t, d), F32),
                 jax.ShapeDtypeStruct((t, d), F32 if router else BF16)]
    if router:
        wr = jnp.zeros((d, LANES), BF16).at[:, :N_EXPERTS].set(w_router.astype(BF16))
        in_specs.append(_resident((d, LANES), lambda i: (0, 0)))
        args.append(wr)
        out_specs += [pl.BlockSpec((tm, LANES), row), pl.BlockSpec((8, tm), lambda i: (0, i))]
        out_shape += [jax.ShapeDtypeStruct((t, LANES), F32), jax.ShapeDtypeStruct((8, t), F32)]
    return pl.pallas_call(
        functools.partial(_outproj_body, router=router),
        grid=(t // tm,),
        in_specs=in_specs, out_specs=out_specs, out_shape=out_shape,
        compiler_params=_params("arbitrary"),
        name="out_proj_router" if router else "out_proj",
    )(*args)


def _ffn_body(h_ref, wg_ref, wu_ref, wd_ref, x1_ref, g2_ref, pg_ref, *rest, nf, n_cast):
    cast_in, o_ref = rest[:n_cast], rest[n_cast]
    cast_out = rest[n_cast + 1:2 * n_cast + 1]
    f = pl.program_id(1)

    @pl.when(f == 0)
    def _():
        o_ref[...] = jnp.zeros_like(o_ref)

    for src, dst in zip(cast_in, cast_out):
        dst[...] = src[...].astype(dst.dtype)

    h = h_ref[...]
    g = jnp.dot(h, wg_ref[...], preferred_element_type=F32)
    u = jnp.dot(h, wu_ref[...], preferred_element_type=F32)
    a = (_silu(g) * u).astype(BF16)
    o_ref[...] += jnp.dot(a, wd_ref[...], preferred_element_type=F32)

    @pl.when(f == nf - 1)
    def _():
        y = o_ref[...]
        o_ref[...] = x1_ref[...] + g2_ref[0] * (y * _rms(y, y.shape[-1]) * pg_ref[...])


def _ffn_tile(f_dim, prefer=(512, 256, 128)):
    for tf in prefer:
        if f_dim % tf == 0:
            return tf
    return f_dim


def _cast_spec(arr, n_i, nf):
    rows, cols = arr.shape
    if rows % n_i == 0 and cols % nf == 0 and (rows // n_i) % 16 == 0 and (cols // nf) % LANES == 0:
        return pl.BlockSpec((rows // n_i, cols // nf), lambda i, f: (i, f))
    if rows % (n_i * nf) == 0 and (rows // (n_i * nf)) % 16 == 0:
        return pl.BlockSpec((rows // (n_i * nf), cols), lambda i, f: (i * nf + f, 0))
    return None


def _dense_ffn(h2, wg, wu, wd, x1, post_g, mod_l, seq, to_cast=()):
    t, d = x1.shape
    f_dim = wg.shape[1]
    tm = min(1024, seq)
    tf = _ffn_tile(f_dim, (256, 128))
    nf = f_dim // tf
    per_seq = seq // tm
    row = lambda i, f: (i, 0)
    cast_specs = [_cast_spec(a, t // tm, nf) for a in to_cast]
    hosted = [a for a, sp in zip(to_cast, cast_specs) if sp is not None]
    specs = [sp for sp in cast_specs if sp is not None]
    outs = pl.pallas_call(
        functools.partial(_ffn_body, nf=nf, n_cast=len(hosted)),
        grid=(t // tm, nf),
        in_specs=[pl.BlockSpec((tm, d), row),
                  pl.BlockSpec((d, tf), lambda i, f: (0, f)),
                  pl.BlockSpec((d, tf), lambda i, f: (0, f)),
                  pl.BlockSpec((tf, d), lambda i, f: (f, 0)),
                  pl.BlockSpec((tm, d), row, pipeline_mode=pl.Buffered(1)),
                  pl.BlockSpec((1, 1, d), lambda i, f: ((i // per_seq) * 6 + 5, 0, 0)),
                  pl.BlockSpec((1, d), lambda i, f: (0, 0))] + specs,
        out_specs=[pl.BlockSpec((tm, d), row)] + specs,
        out_shape=[jax.ShapeDtypeStruct((t, d), F32)]
        + [jax.ShapeDtypeStruct(a.shape, BF16) for a in hosted],
        compiler_params=_params("arbitrary", "arbitrary"),
        name="dense_ffn",
    )(h2, wg, wu, wd, x1, mod_l, post_g.reshape(1, d), *hosted)
    done = iter(outs[1:])
    return outs[0], [next(done) if sp is not None else a.astype(BF16)
                     for a, sp in zip(to_cast, cast_specs)]


def _wait_slot(buf_ref, sem):
    pltpu.make_async_copy(buf_ref, buf_ref, sem).wait()


def _moe_body(te_ref, nu_ref, rows_ref, nrows_ref, h_hbm, wg_ref, wu_ref, wd_ref, y_ref,
              xg_ref, hb_ref, acc_ref, sem, *, tm, cs, nc, nf):
    del te_ref
    i = pl.program_id(0)
    f = pl.program_id(1)
    n_used = nu_ref[0]
    used = i < n_used
    slot = lax.rem(i, 2)

    def start_row(idx_ref, chunk, j, dst_slot):
        tok = idx_ref[0, 0, chunk * cs + j]
        pltpu.make_async_copy(h_hbm.at[pl.ds(tok, 1)], xg_ref.at[dst_slot, chunk, pl.ds(j, 1)],
                              sem.at[dst_slot]).start()

    @pl.when((i == 0) & (f == 0))
    def _():
        def chunk_body(c, carry):
            for j in range(cs):
                start_row(rows_ref, c, j, 0)
            return carry
        lax.fori_loop(0, nc, chunk_body, 0)

    @pl.when((i <= n_used) & (f == 0))
    def _():
        _wait_slot(xg_ref.at[slot], sem.at[slot])

    @pl.when(used & (f == 0))
    def _():
        for c in range(nc):
            n = min(cs, tm - c * cs)
            if n > 0:
                hb_ref[c * cs:c * cs + n, :] = xg_ref[slot, c, :n, :].astype(BF16)
        acc_ref[...] = jnp.zeros_like(acc_ref)

    def ffn_step(issue):
        if issue:
            for j in range(cs):
                start_row(nrows_ref, f, j, 1 - slot)
        h = hb_ref[...]
        g = jnp.dot(h, wg_ref[0], preferred_element_type=F32)
        u = jnp.dot(h, wu_ref[0], preferred_element_type=F32)
        a = (_silu(g) * u).astype(BF16)
        acc_ref[...] += jnp.dot(a, wd_ref[0], preferred_element_type=F32)

    pl.when(used & (f < nc))(functools.partial(ffn_step, True))
    if nc < nf:
        pl.when(used & (f >= nc))(functools.partial(ffn_step, False))

    @pl.when(used & (f == nf - 1))
    def _():
        y_ref[...] = acc_ref[...]

    @pl.when(jnp.logical_not(used) & (f == nf - 1))
    def _():
        y_ref[...] = jnp.zeros_like(y_ref)


def _combine_body(p0_ref, p1_ref, np0_ref, np1_ref, y_hbm, rout_ref, x1_ref, g2_ref, pg_ref, o_ref,
                  yb_ref, sem, *, tc, n_steps):
    i = pl.program_id(0)
    slot = lax.rem(i, 2)
    d = o_ref.shape[-1]

    def start_rows(idx_refs, dst_slot):
        def body(it, carry):
            for j in range(8):
                for k in range(2):
                    p = idx_refs[k][0, 0, it * 8 + j]
                    pltpu.make_async_copy(y_hbm.at[pl.ds(p, 1)],
                                          yb_ref.at[dst_slot, k, it, pl.ds(j, 1)],
                                          sem.at[dst_slot]).start()
            return carry
        lax.fori_loop(0, tc // 8, body, 0)

    @pl.when(i == 0)
    def _():
        start_rows((p0_ref, p1_ref), 0)

    @pl.when(i + 1 < n_steps)
    def _():
        start_rows((np0_ref, np1_ref), 1 - slot)

    _wait_slot(yb_ref.at[slot], sem.at[slot])
    w = rout_ref[...]
    y = (w[:, 2:3] * yb_ref[slot, 0].reshape(tc, d) + w[:, 3:4] * yb_ref[slot, 1].reshape(tc, d))
    o_ref[...] = x1_ref[...] + g2_ref[0] * (y * _rms(y, d) * pg_ref[...])


def _moe_ffn(h2, rout, ridx, wg, wu, wd, x1, post_g, mod_l, seq):
    t, d = x1.shape
    f_dim = wg.shape[2]
    tm = min(512, seq)
    tf = _ffn_tile(f_dim)
    nf = f_dim // tf
    n_pairs = 2 * t
    n_tiles = n_pairs // tm + N_EXPERTS + 1
    n_rows = n_tiles * tm
    nc = min(nf, 8)
    cs = -(-tm // (nc * 16)) * 16
    tg = nc * cs

    expert = ridx[:2].astype(jnp.int32).reshape(1, n_pairs)
    onehot = (expert == jnp.arange(N_EXPERTS, dtype=jnp.int32)[:, None]).astype(jnp.int32)
    blocks = n_pairs // LANES
    tri = (jnp.arange(LANES)[:, None] <= jnp.arange(LANES)[None, :]).astype(F32)
    within = jnp.dot(onehot.reshape(N_EXPERTS * blocks, LANES).astype(F32), tri,
                     preferred_element_type=F32).astype(jnp.int32).reshape(N_EXPERTS, blocks, LANES)
    block_tot = within[:, :, -1]
    csum = (within + (jnp.cumsum(block_tot, axis=1) - block_tot)[:, :, None]).reshape(N_EXPERTS, n_pairs)
    rank = jnp.sum(onehot * csum, axis=0) - 1
    tiles_per = (csum[:, -1] + tm - 1) // tm
    tile_end = jnp.cumsum(tiles_per)
    tile_start = tile_end - tiles_per
    n_used = tile_end[-1]
    pos = jnp.sum(onehot * tile_start[:, None], axis=0) * tm + rank
    token = jnp.tile(jnp.arange(t, dtype=jnp.int32), 2)
    rows = jnp.zeros((n_rows,), jnp.int32).at[pos].set(token)
    tile_ids = jnp.minimum(jnp.arange(n_tiles, dtype=jnp.int32), n_used - 1)
    tile_expert = jnp.sum((tile_ids[:, None] >= tile_end[None, :]).astype(jnp.int32), axis=1)

    rows = jnp.pad(rows.reshape(n_tiles, tm), ((0, 0), (0, tg - tm))).reshape(n_tiles, 1, tg)

    fidx = lambda i, f, te, nu: jnp.where(i < nu[0], f, nf - 1)
    y_sorted = pl.pallas_call(
        functools.partial(_moe_body, tm=tm, cs=cs, nc=nc, nf=nf),
        grid_spec=pltpu.PrefetchScalarGridSpec(
            num_scalar_prefetch=2,
            grid=(n_tiles, nf),
            in_specs=[pl.BlockSpec((1, 1, tg), lambda i, f, te, nu: (i, 0, 0),
                                   memory_space=pltpu.SMEM),
                      pl.BlockSpec((1, 1, tg),
                                   lambda i, f, te, nu: (jnp.minimum(i + 1, n_tiles - 1), 0, 0),
                                   memory_space=pltpu.SMEM),
                      pl.BlockSpec(memory_space=pl.ANY),
                      pl.BlockSpec((1, d, tf), lambda i, f, te, nu: (te[i], 0, fidx(i, f, te, nu))),
                      pl.BlockSpec((1, d, tf), lambda i, f, te, nu: (te[i], 0, fidx(i, f, te, nu))),
                      pl.BlockSpec((1, tf, d), lambda i, f, te, nu: (te[i], fidx(i, f, te, nu), 0))],
            out_specs=pl.BlockSpec((tm, d), lambda i, f, te, nu: (i, 0)),
            scratch_shapes=[pltpu.VMEM((2, nc, cs, d), F32), pltpu.VMEM((tm, d), BF16),
                            pltpu.VMEM((tm, d), F32), pltpu.SemaphoreType.DMA((2,))]),
        out_shape=jax.ShapeDtypeStruct((n_rows, d), F32),
        compiler_params=_params("arbitrary", "arbitrary"),
        name="expert_ffn",
    )(tile_expert, n_used.reshape(1).astype(jnp.int32), rows, rows, h2, wg, wu, wd)

    tc = min(256, seq)
    per_seq = seq // tc
    n_steps = t // tc
    row = lambda i: (i, 0)
    pos = pos.reshape(2, n_steps, 1, tc)
    cur = pl.BlockSpec((1, 1, tc), lambda i: (i, 0, 0), memory_space=pltpu.SMEM)
    nxt = pl.BlockSpec((1, 1, tc), lambda i: (jnp.minimum(i + 1, n_steps - 1), 0, 0),
                       memory_space=pltpu.SMEM)
    return pl.pallas_call(
        functools.partial(_combine_body, tc=tc, n_steps=n_steps),
        grid=(n_steps,),
        in_specs=[cur, cur, nxt, nxt,
                  pl.BlockSpec(memory_space=pl.ANY),
                  pl.BlockSpec((tc, LANES), row),
                  pl.BlockSpec((tc, d), row),
                  pl.BlockSpec((1, 1, d), lambda i: ((i // per_seq) * 6 + 5, 0, 0)),
                  pl.BlockSpec((1, d), lambda i: (0, 0))],
        out_specs=pl.BlockSpec((tc, d), row),
        out_shape=jax.ShapeDtypeStruct((t, d), F32),
        scratch_shapes=[pltpu.VMEM((2, 2, tc // 8, 8, d), F32), pltpu.SemaphoreType.DMA((2,))],
        compiler_params=_params("arbitrary"),
        name="expert_combine",
    )(pos[0], pos[1], pos[0], pos[1], y_sorted, rout, x1, mod_l, post_g.reshape(1, d))


def kernel(x, c, w_mod, b_mod, pre_mix_g, post_mix_g, pre_ffn_g, post_ffn_g, w_in, q_norm_g, k_norm_g, sinks, grp_norm_a, grp_norm_b, w_out, w_ffn_gate, w_ffn_up, w_ffn_down, w_router, w_exp_gate, w_exp_up, w_exp_down):
    batch, seq, d = x.shape
    n_layers = w_mod.shape[0]
    assert seq % 256 == 0 or seq < 256, seq
    assert w_in.shape[2] == IN_COLS and w_out.shape[1] == 2 * Q_W

    mod = _modulation(c, w_mod, b_mod)
    tables = _rope_tables(seq)
    xf = x.reshape(batch * seq, d)
    expert_w = None
    for l in range(n_layers):
        mod_l = mod[l].reshape(batch * 6, 1, d)
        qkv = _qkv(xf, mod_l, pre_mix_g[l], w_in[l].astype(BF16), q_norm_g[l], k_norm_g[l],
                   tables, seq)
        merged = _attention(qkv, sinks[l], grp_norm_a[l], grp_norm_b[l], batch, seq)
        i = l // 2
        if l % 2 == 0:
            x1, h2 = _outproj(merged, w_out[l].astype(BF16), xf, post_mix_g[l], pre_ffn_g[l],
                              mod_l, seq)
            to_cast = ()
            if l + 1 < n_layers:
                to_cast = [w.reshape(-1, w.shape[-1])
                           for w in (w_exp_gate[i], w_exp_up[i], w_exp_down[i])]
            xf, cast = _dense_ffn(h2, w_ffn_gate[i].astype(BF16), w_ffn_up[i].astype(BF16),
                                  w_ffn_down[i].astype(BF16), x1, post_ffn_g[l], mod_l, seq,
                                  to_cast=to_cast)
            if cast:
                expert_w = [c2.reshape(w.shape[1:]) for c2, w in
                            zip(cast, (w_exp_gate, w_exp_up, w_exp_down))]
        else:
            x1, h2, rout, ridx = _outproj(merged, w_out[l].astype(BF16), xf, post_mix_g[l],
                                    pre_ffn_g[l], mod_l, seq, w_router=w_router[i])
            if expert_w is None:
                expert_w = [w[i].astype(BF16) for w in (w_exp_gate, w_exp_up, w_exp_down)]
            xf = _moe_ffn(h2, rout, ridx, *expert_w, x1, post_ffn_g[l], mod_l, seq)
            expert_w = None
    return xf.reshape(batch, seq, d)
```

```python
import functools

import jax
import jax.numpy as jnp
from jax import lax
from jax.experimental import pallas as pl
from jax.experimental.pallas import tpu as pltpu

F32 = jnp.float32
BF16 = jnp.bfloat16

HEAD_DIM = 128
N_Q = 8
N_KV = 2
GQA = N_Q // N_KV
Q_W = N_Q * HEAD_DIM
KV_W = N_KV * HEAD_DIM
IN_COLS = 2 * (Q_W + 2 * KV_W)
WINDOW = 128
GRID_W = 64
ROPE_THETA = 10000.0
N_EXPERTS = 8
EPS = 1e-6
LOG2_E = 1.4426950408889634
HEADS_PER_JOB_A = 2
HEADS_PER_JOB_B = 4
OUTPROJ_ROWS = 256
WEIGHT_RING = 3
JOBS_AHEAD = 2
LANES = 128
V7X_VMEM_LIMIT = 56 * 1024 * 1024


def _params(*sem):
    return pltpu.CompilerParams(dimension_semantics=sem, vmem_limit_bytes=V7X_VMEM_LIMIT)


def _resident(shape, index_map):
    return pl.BlockSpec(shape, index_map, pipeline_mode=pl.Buffered(1))


def _rms(v, width):
    return lax.rsqrt(jnp.sum(v * v, axis=-1, keepdims=True) * (1.0 / width) + EPS)


def _silu(v):
    return v * jax.nn.sigmoid(v)


def _mod_body(c_ref, w_ref, b_ref, o_ref):
    ca = _silu(c_ref[...]).astype(BF16)
    o_ref[0] = jnp.dot(ca, w_ref[0].astype(BF16), preferred_element_type=F32) + b_ref[0]


def _modulation(c, w_mod, b_mod):
    n_layers, d, n = w_mod.shape
    b = c.shape[0]
    tn = next(c for c in (1024, 512, 256, LANES) if n % c == 0)
    return pl.pallas_call(
        _mod_body,
        grid=(n_layers, n // tn),
        in_specs=[pl.BlockSpec((b, d), lambda l, j: (0, 0)),
                  pl.BlockSpec((1, d, tn), lambda l, j: (l, 0, j)),
                  pl.BlockSpec((1, 1, tn), lambda l, j: (l, 0, j))],
        out_specs=pl.BlockSpec((1, b, tn), lambda l, j: (l, 0, j)),
        out_shape=jax.ShapeDtypeStruct((n_layers, b, n), F32),
        compiler_params=_params("arbitrary", "arbitrary"),
        name="modulation",
    )(c, w_mod, b_mod.reshape(n_layers, 1, n))


def _rope_tables(seq):
    def angles(pos, dim):
        inv = 1.0 / (ROPE_THETA ** (jnp.arange(0, dim, 2, dtype=F32) / dim))
        return pos.astype(F32)[:, None] * inv[None, :]
    pos = jnp.arange(seq)
    a_row = angles(pos // GRID_W, HEAD_DIM // 2)
    a_col = angles(pos % GRID_W, HEAD_DIM // 2)
    a_seq = angles(pos, HEAD_DIM)
    cos_a = jnp.concatenate([jnp.cos(a_row)] * 2 + [jnp.cos(a_col)] * 2, axis=-1)
    sin_a = jnp.concatenate([-jnp.sin(a_row), jnp.sin(a_row), -jnp.sin(a_col), jnp.sin(a_col)], axis=-1)
    cos_b = jnp.concatenate([jnp.cos(a_seq)] * 2, axis=-1)
    sin_b = jnp.concatenate([-jnp.sin(a_seq), jnp.sin(a_seq)], axis=-1)
    return cos_a, sin_a, cos_b, sin_b


def _head_plan():
    plan = []
    for s in range(IN_COLS // HEAD_DIM):
        if s < 8:
            plan.append(("qa", s))
        elif s < 10:
            plan.append(("ka", s + 8))
        elif s < 12:
            plan.append(("v", s + 8))
        elif s < 20:
            plan.append(("qb", s - 4))
        elif s < 22:
            plan.append(("kb", s))
        else:
            plan.append(("v", s))
    return plan


def _qkv_body(x_ref, g_ref, sh_ref, sc_ref, w_ref, qg_ref, kg_ref,
              ca_ref, sa_ref, cb_ref, sb_ref, o_ref):
    x = x_ref[...]
    d = x.shape[-1]
    xn = x * _rms(x, d) * g_ref[...]
    h = (xn * (1.0 + sc_ref[0]) + sh_ref[0]).astype(BF16)

    tm = x.shape[0]
    lane = lax.broadcasted_iota(jnp.int32, (tm, HEAD_DIM), 1)
    first_quarter = (lane & (HEAD_DIM // 2 - 1)) < (HEAD_DIM // 4)
    ca, sa, cb, sb = ca_ref[...], sa_ref[...], cb_ref[...], sb_ref[...]
    scale = HEAD_DIM ** -0.5 * LOG2_E

    def norm(a, g):
        return a * _rms(a, HEAD_DIM) * g

    def rope_a(a):
        partner = jnp.where(first_quarter, pltpu.roll(a, 96, 1), pltpu.roll(a, 32, 1))
        return a * ca + partner * sa

    def rope_b(a):
        return a * cb + pltpu.roll(a, 64, 1) * sb

    plan = _head_plan()
    chunk = 2 * HEAD_DIM
    for c in range(IN_COLS // chunk):
        acc = jnp.dot(h, w_ref[:, c * chunk:(c + 1) * chunk], preferred_element_type=F32)
        for half in range(2):
            kind, dst = plan[2 * c + half]
            a = acc[:, half * HEAD_DIM:(half + 1) * HEAD_DIM]
            if kind == "qa":
                a = rope_a(norm(a, qg_ref[...])) * scale
            elif kind == "ka":
                a = rope_a(norm(a, kg_ref[...]))
            elif kind == "qb":
                a = rope_b(a) * scale
            elif kind == "kb":
                a = rope_b(a)
            o_ref[:, dst * HEAD_DIM:(dst + 1) * HEAD_DIM] = a.astype(o_ref.dtype)


def _qkv(xf, mod_l, pre_g, w_in, qg, kg, tables, seq):
    t, d = xf.shape
    tm = min(512, seq)
    per_seq = seq // tm
    row = lambda i: (i, 0)
    tab = lambda i: (i % per_seq, 0)
    vec = pl.BlockSpec((1, d), lambda i: (0, 0))
    hvec = pl.BlockSpec((1, HEAD_DIM), lambda i: (0, 0))
    tspec = pl.BlockSpec((tm, HEAD_DIM), tab)
    return pl.pallas_call(
        _qkv_body,
        grid=(t // tm,),
        in_specs=[pl.BlockSpec((tm, d), row), vec,
                  pl.BlockSpec((1, 1, d), lambda i: ((i // per_seq) * 6 + 0, 0, 0)),
                  pl.BlockSpec((1, 1, d), lambda i: ((i // per_seq) * 6 + 1, 0, 0)),
                  _resident((d, IN_COLS), lambda i: (0, 0)),
                  hvec, hvec, tspec, tspec, tspec, tspec],
        out_specs=pl.BlockSpec((tm, IN_COLS), row),
        out_shape=jax.ShapeDtypeStruct((t, IN_COLS), BF16),
        compiler_params=_params("arbitrary"),
        name="qkv_proj",
    )(xf, pre_g.reshape(1, d), mod_l, mod_l, w_in, qg.reshape(1, HEAD_DIM), kg.reshape(1, HEAD_DIM),
      *tables)


ONES_ROWS = 16


def _values_t(v):
    vt = v.astype(F32).T.astype(BF16)
    return jnp.concatenate([vt, jnp.ones((ONES_ROWS, vt.shape[1]), BF16)], axis=0)


def _reduce_rows(fn, v):
    rows, n = v.shape
    if rows % 128 == 0 and rows > 128:
        v = fn(v.reshape(rows // 128, 128, n), axis=0)
    return fn(v, axis=0, keepdims=True)


def _attn_body(sinks_ref, qa_ref, qb_ref, kva_ref, kvb_ref, gna_ref, gnb_ref, o_ref,
               vt_ref, acc_ref, *, tq, seq, wk):
    i = pl.program_id(1)
    nt = (((1,), (1,)), ((), ()))

    @pl.when(i == 0)
    def _():
        for kv in range(N_KV):
            vt_ref[kv] = _values_t(kva_ref[:, KV_W + kv * HEAD_DIM:KV_W + (kv + 1) * HEAD_DIM])

    def stack_q(q_ref, hd0, n):
        qs = [q_ref[:, hd * HEAD_DIM:(hd + 1) * HEAD_DIM] for hd in range(hd0, hd0 + n)]
        return qs[0] if n == 1 else jnp.concatenate(qs, axis=0)

    def scores_a(hd0, n):
        k = kva_ref[:, (hd0 // GQA) * HEAD_DIM:(hd0 // GQA + 1) * HEAD_DIM]
        return lax.dot_general(k, stack_q(qa_ref, hd0, n), nt, preferred_element_type=F32)

    def finish_a(hd0, n, s):
        m = _reduce_rows(jnp.max, s)
        p = jnp.exp2(s - m).astype(BF16)
        o = jnp.dot(vt_ref[hd0 // GQA], p, preferred_element_type=F32)
        o = o[:HEAD_DIM] / o[HEAD_DIM:HEAD_DIM + 1]
        for j in range(n):
            acc_ref[hd0 + j] = o[:, j * tq:(j + 1) * tq]

    start = pl.multiple_of(jnp.clip(i * tq - WINDOW, 0, seq - wk), WINDOW)
    kpos = start + lax.broadcasted_iota(jnp.int32, (wk, tq), 0)
    qpos = i * tq + lax.broadcasted_iota(jnp.int32, (wk, tq), 1)
    bias = jnp.where(jnp.abs(kpos - qpos) <= WINDOW, 0.0, -jnp.inf)
    vt_b = [_values_t(kvb_ref[pl.ds(start, wk), KV_W + kv * HEAD_DIM:KV_W + (kv + 1) * HEAD_DIM])
            for kv in range(N_KV)]

    def scores_b(hd0, n):
        k = kvb_ref[pl.ds(start, wk), (hd0 // GQA) * HEAD_DIM:(hd0 // GQA + 1) * HEAD_DIM]
        s = lax.dot_general(k, stack_q(qb_ref, hd0, n), nt, preferred_element_type=F32)
        return s + (bias if n == 1 else jnp.concatenate([bias] * n, axis=1))

    def finish_b(hd0, n, s):
        sinks = [jnp.full((1, tq), sinks_ref[hd0 + j] * LOG2_E, F32) for j in range(n)]
        sink = sinks[0] if n == 1 else jnp.concatenate(sinks, axis=1)
        m = jnp.maximum(_reduce_rows(jnp.max, s), sink)
        e = jnp.exp2(s - m).astype(BF16)
        o = jnp.dot(vt_b[hd0 // GQA], e, preferred_element_type=F32)
        o = o[:HEAD_DIM] / (o[HEAD_DIM:HEAD_DIM + 1] + jnp.exp2(sink - m))
        for j in range(n):
            acc_ref[N_Q + hd0 + j] = o[:, j * tq:(j + 1) * tq]

    jobs = ([(scores_a, finish_a, hd0, HEADS_PER_JOB_A) for hd0 in range(0, N_Q, HEADS_PER_JOB_A)]
            + [(scores_b, finish_b, hd0, HEADS_PER_JOB_B) for hd0 in range(0, N_Q, HEADS_PER_JOB_B)])
    ahead = min(JOBS_AHEAD, len(jobs))
    pending = [jobs[j][0](jobs[j][2], jobs[j][3]) for j in range(ahead)]
    for j, (_, finish, hd0, n) in enumerate(jobs):
        if j + ahead < len(jobs):
            nxt = jobs[j + ahead]
            pending.append(nxt[0](nxt[2], nxt[3]))
        finish(hd0, n, pending.pop(0))

    for grp, gain_ref in enumerate((gna_ref, gnb_ref)):
        ss = jnp.zeros((1, tq), F32)
        for hd in range(N_Q):
            o = acc_ref[grp * N_Q + hd]
            ss = ss + jnp.sum(o * o, axis=0, keepdims=True)
        r = lax.rsqrt(ss * (1.0 / Q_W) + EPS)
        for hd in range(N_Q):
            col = grp * Q_W + hd * HEAD_DIM
            o = (acc_ref[grp * N_Q + hd] * r).T * gain_ref[:, hd * HEAD_DIM:(hd + 1) * HEAD_DIM]
            o_ref[:, col:col + HEAD_DIM] = o.astype(o_ref.dtype)


def _attention(qkv, sinks, gna, gnb, batch, seq):
    t = qkv.shape[0]
    tq = min(256, seq)
    wk = min(tq + 2 * WINDOW, seq)
    per_seq = seq // tq
    row = lambda b, i: (b * per_seq + i, 0)
    body = functools.partial(_attn_body, tq=tq, seq=seq, wk=wk)
    gvec = pl.BlockSpec((1, Q_W), lambda b, i: (0, 0))
    return pl.pallas_call(
        body,
        grid=(batch, per_seq),
        in_specs=[pl.BlockSpec(memory_space=pltpu.SMEM),
                  pl.BlockSpec((tq, Q_W), lambda b, i: (b * per_seq + i, 0)),
                  pl.BlockSpec((tq, Q_W), lambda b, i: (b * per_seq + i, 1)),
                  pl.BlockSpec((seq, 2 * KV_W), lambda b, i: (b, 4)),
                  pl.BlockSpec((seq, 2 * KV_W), lambda b, i: (b, 5)),
                  gvec, gvec],
        out_specs=pl.BlockSpec((tq, 2 * Q_W), row),
        out_shape=jax.ShapeDtypeStruct((t, 2 * Q_W), BF16),
        scratch_shapes=[pltpu.VMEM((N_KV, HEAD_DIM + ONES_ROWS, seq), BF16),
                        pltpu.VMEM((2 * N_Q, HEAD_DIM, tq), F32)],
        compiler_params=_params("arbitrary", "arbitrary"),
        name="attention",
    )(sinks, qkv, qkv, qkv, qkv, gna.reshape(1, Q_W), gnb.reshape(1, Q_W))


def _outproj_body(*refs, router):
    if router:
        (m_ref, w_ref, x_ref, pg_ref, g1_ref, fg_ref, sh2_ref, sc2_ref, wr_ref,
         x1_ref, h2_ref, rout_ref, ridx_ref) = refs
    else:
        m_ref, w_ref, x_ref, pg_ref, g1_ref, fg_ref, sh2_ref, sc2_ref, x1_ref, h2_ref = refs
    tm, d = x_ref.shape
    rows = min(tm, OUTPROJ_ROWS)

    def project(r):
        return jnp.dot(m_ref[r * rows:(r + 1) * rows, :], w_ref[...], preferred_element_type=F32)

    def epilogue(r, y):
        sl = slice(r * rows, (r + 1) * rows)
        x1 = x_ref[sl, :] + g1_ref[0] * (y * _rms(y, d) * pg_ref[...])
        x1_ref[sl, :] = x1
        h2 = (x1 * _rms(x1, d) * fg_ref[...]) * (1.0 + sc2_ref[0]) + sh2_ref[0]
        h2_ref[sl, :] = h2.astype(h2_ref.dtype)
        if router:
            logits = jnp.dot(h2.astype(BF16), wr_ref[...], preferred_element_type=F32)
            lane = lax.broadcasted_iota(jnp.int32, logits.shape, 1).astype(F32)
            lg = jnp.where(lane < N_EXPERTS, logits, -jnp.inf)
            m1 = jnp.max(lg, axis=-1, keepdims=True)
            i1 = jnp.min(jnp.where(lg == m1, lane, float(LANES)), axis=-1, keepdims=True)
            lg2 = jnp.where(lane == i1, -jnp.inf, lg)
            m2 = jnp.max(lg2, axis=-1, keepdims=True)
            i2 = jnp.min(jnp.where(lg2 == m2, lane, float(LANES)), axis=-1, keepdims=True)
            e2 = jnp.exp(m2 - m1)
            den = 1.0 + e2
            rout = jnp.where(lane == 0, i1, jnp.where(lane == 1, i2, jnp.where(
                lane == 2, 1.0 / den, jnp.where(lane == 3, e2 / den, 0.0))))
            rout_ref[sl, :] = rout
            ridx_ref[:, sl] = rout.T[:8, :]

    y = project(0)
    for r in range(tm // rows):
        y_next = project(r + 1) if r + 1 < tm // rows else None
        epilogue(r, y)
        y = y_next


def _outproj(merged, w_out, xf, post_g, ffn_g, mod_l, seq, w_router=None):
    t, d = xf.shape
    k = merged.shape[1]
    tm = min(512, seq)
    per_seq = seq // tm
    row = lambda i: (i, 0)
    vec = pl.BlockSpec((1, d), lambda i: (0, 0))
    modv = lambda j: pl.BlockSpec((1, 1, d), lambda i: ((i // per_seq) * 6 + j, 0, 0))
    router = w_router is not None
    in_specs = [pl.BlockSpec((tm, k), row), _resident((k, d), lambda i: (0, 0)),
                pl.BlockSpec((tm, d), row), vec, modv(2), vec, modv(3), modv(4)]
    args = [merged, w_out, xf, post_g.reshape(1, d), mod_l, ffn_g.reshape(1, d), mod_l, mod_l]
    out_specs = [pl.BlockSpec((tm, d), row), pl.BlockSpec((tm, d), row)]
    out_shape = [jax.ShapeDtypeStruct((t, d), F32),
                 jax.ShapeDtypeStruct((t, d), F32 if router else BF16)]
    if router:
        wr = jnp.zeros((d, LANES), BF16).at[:, :N_EXPERTS].set(w_router.astype(BF16))
        in_specs.append(_resident((d, LANES), lambda i: (0, 0)))
        args.append(wr)
        out_specs += [pl.BlockSpec((tm, LANES), row), pl.BlockSpec((8, tm), lambda i: (0, i))]
        out_shape += [jax.ShapeDtypeStruct((t, LANES), F32), jax.ShapeDtypeStruct((8, t), F32)]
    return pl.pallas_call(
        functools.partial(_outproj_body, router=router),
        grid=(t // tm,),
        in_specs=in_specs, out_specs=out_specs, out_shape=out_shape,
        compiler_params=_params("arbitrary"),
        name="out_proj_router" if router else "out_proj",
    )(*args)


def _ring_copies(srcs, bufs, sems, slot):
    return [pltpu.make_async_copy(src, buf.at[slot], sems.at[k, slot])
            for k, (src, buf) in enumerate(zip(srcs, bufs))]


def _ffn_body(h_ref, wg_hbm, wu_hbm, wd_hbm, x1_ref, g2_ref, pg_ref, *rest, nf, n_steps, tf, n_cast):
    cast_in, o_ref = rest[:n_cast], rest[n_cast]
    cast_out = rest[n_cast + 1:2 * n_cast + 1]
    wg_buf, wu_buf, wd_buf, wsem = rest[2 * n_cast + 1:]
    f = pl.program_id(1)
    step = pl.program_id(0) * nf + f
    slot = lax.rem(step, WEIGHT_RING)

    def chunk_copies(chunk, dst_slot):
        cols = pl.ds(pl.multiple_of(chunk * tf, tf), tf)
        return _ring_copies((wg_hbm.at[:, cols], wu_hbm.at[:, cols], wd_hbm.at[cols, :]),
                            (wg_buf, wu_buf, wd_buf), wsem, dst_slot)

    @pl.when(step == 0)
    def _():
        for k in range(WEIGHT_RING - 1):
            for c in chunk_copies(k % nf, k):
                c.start()

    @pl.when(step + (WEIGHT_RING - 1) < n_steps)
    def _():
        ahead = step + (WEIGHT_RING - 1)
        for c in chunk_copies(lax.rem(ahead, nf), lax.rem(ahead, WEIGHT_RING)):
            c.start()

    for c in chunk_copies(f, slot):
        c.wait()

    @pl.when(f == 0)
    def _():
        o_ref[...] = jnp.zeros_like(o_ref)

    for src, dst in zip(cast_in, cast_out):
        dst[...] = src[...].astype(dst.dtype)

    h = h_ref[...]
    g = jnp.dot(h, wg_buf[slot], preferred_element_type=F32)
    u = jnp.dot(h, wu_buf[slot], preferred_element_type=F32)
    a = (_silu(g) * u).astype(BF16)
    o_ref[...] += jnp.dot(a, wd_buf[slot], preferred_element_type=F32)

    @pl.when(f == nf - 1)
    def _():
        y = o_ref[...]
        o_ref[...] = x1_ref[...] + g2_ref[0] * (y * _rms(y, y.shape[-1]) * pg_ref[...])


def _weight_ring_scratch(d, tf):
    return [pltpu.VMEM((WEIGHT_RING, d, tf), BF16), pltpu.VMEM((WEIGHT_RING, d, tf), BF16),
            pltpu.VMEM((WEIGHT_RING, tf, d), BF16), pltpu.SemaphoreType.DMA((3, WEIGHT_RING))]


def _ffn_tile(f_dim, prefer=(512, 256, 128)):
    for tf in prefer:
        if f_dim % tf == 0:
            return tf
    return f_dim


def _cast_spec(arr, n_i, nf):
    rows, cols = arr.shape
    if rows % n_i == 0 and cols % nf == 0 and (rows // n_i) % 16 == 0 and (cols // nf) % LANES == 0:
        return pl.BlockSpec((rows // n_i, cols // nf), lambda i, f: (i, f))
    if rows % (n_i * nf) == 0 and (rows // (n_i * nf)) % 16 == 0:
        return pl.BlockSpec((rows // (n_i * nf), cols), lambda i, f: (i * nf + f, 0))
    return None


def _dense_ffn(h2, wg, wu, wd, x1, post_g, mod_l, seq, to_cast=()):
    t, d = x1.shape
    f_dim = wg.shape[1]
    tm = min(512, seq)
    tf = _ffn_tile(f_dim)
    nf = f_dim // tf
    n_i = t // tm
    per_seq = seq // tm
    row = lambda i, f: (i, 0)
    cast_specs = [_cast_spec(a, t // tm, nf) for a in to_cast]
    hosted = [a for a, sp in zip(to_cast, cast_specs) if sp is not None]
    specs = [sp for sp in cast_specs if sp is not None]
    outs = pl.pallas_call(
        functools.partial(_ffn_body, nf=nf, n_steps=n_i * nf, tf=tf, n_cast=len(hosted)),
        grid=(n_i, nf),
        in_specs=[pl.BlockSpec((tm, d), row),
                  pl.BlockSpec(memory_space=pl.ANY),
                  pl.BlockSpec(memory_space=pl.ANY),
                  pl.BlockSpec(memory_space=pl.ANY),
                  pl.BlockSpec((tm, d), row),
                  pl.BlockSpec((1, 1, d), lambda i, f: ((i // per_seq) * 6 + 5, 0, 0)),
                  pl.BlockSpec((1, d), lambda i, f: (0, 0))] + specs,
        out_specs=[pl.BlockSpec((tm, d), row)] + specs,
        out_shape=[jax.ShapeDtypeStruct((t, d), F32)]
        + [jax.ShapeDtypeStruct(a.shape, BF16) for a in hosted],
        scratch_shapes=_weight_ring_scratch(d, tf),
        compiler_params=_params("arbitrary", "arbitrary"),
        name="dense_ffn",
    )(h2, wg, wu, wd, x1, mod_l, post_g.reshape(1, d), *hosted)
    done = iter(outs[1:])
    return outs[0], [next(done) if sp is not None else a.astype(BF16)
                     for a, sp in zip(to_cast, cast_specs)]


def _wait_slot(buf_ref, sem):
    pltpu.make_async_copy(buf_ref, buf_ref, sem).wait()


def _moe_body(te_ref, nu_ref, rows_ref, nrows_ref, h_hbm, wg_hbm, wu_hbm, wd_hbm, y_ref,
              xg_ref, hb_ref, acc_ref, sem, wg_buf, wu_buf, wd_buf, wsem, *, tm, tf, cs, nc, nf):
    i = pl.program_id(0)
    f = pl.program_id(1)
    n_used = nu_ref[0]
    used = i < n_used
    slot = lax.rem(i, 2)
    step = i * nf + f
    n_steps = n_used * nf
    wslot = lax.rem(step, WEIGHT_RING)

    def chunk_copies(tile, chunk, dst_slot):
        e = te_ref[tile]
        cols = pl.ds(pl.multiple_of(chunk * tf, tf), tf)
        return _ring_copies((wg_hbm.at[e, :, cols], wu_hbm.at[e, :, cols], wd_hbm.at[e, cols, :]),
                            (wg_buf, wu_buf, wd_buf), wsem, dst_slot)

    @pl.when(step == 0)
    def _():
        for k in range(WEIGHT_RING - 1):
            for c in chunk_copies(k // nf, k % nf, k):
                c.start()

    @pl.when(step + (WEIGHT_RING - 1) < n_steps)
    def _():
        ahead = step + (WEIGHT_RING - 1)
        tile = ahead // nf
        for c in chunk_copies(tile, ahead - tile * nf, lax.rem(ahead, WEIGHT_RING)):
            c.start()

    @pl.when(used)
    def _():
        for c in chunk_copies(i, f, wslot):
            c.wait()

    def start_row(idx_ref, chunk, j, dst_slot):
        tok = idx_ref[0, 0, chunk * cs + j]
        pltpu.make_async_copy(h_hbm.at[pl.ds(tok, 1)], xg_ref.at[dst_slot, chunk, pl.ds(j, 1)],
                              sem.at[dst_slot]).start()

    @pl.when((i == 0) & (f == 0))
    def _():
        def chunk_body(c, carry):
            for j in range(cs):
                start_row(rows_ref, c, j, 0)
            return carry
        lax.fori_loop(0, nc, chunk_body, 0)

    @pl.when((i <= n_used) & (f == 0))
    def _():
        _wait_slot(xg_ref.at[slot], sem.at[slot])

    @pl.when(used & (f == 0))
    def _():
        for c in range(nc):
            n = min(cs, tm - c * cs)
            if n > 0:
                hb_ref[c * cs:c * cs + n, :] = xg_ref[slot, c, :n, :].astype(BF16)
        acc_ref[...] = jnp.zeros_like(acc_ref)

    def ffn_step(issue):
        if issue:
            for j in range(cs):
                start_row(nrows_ref, f, j, 1 - slot)
        h = hb_ref[...]
        g = jnp.dot(h, wg_buf[wslot], preferred_element_type=F32)
        u = jnp.dot(h, wu_buf[wslot], preferred_element_type=F32)
        a = (_silu(g) * u).astype(BF16)
        acc_ref[...] += jnp.dot(a, wd_buf[wslot], preferred_element_type=F32)

    pl.when(used & (f < nc))(functools.partial(ffn_step, True))
    if nc < nf:
        pl.when(used & (f >= nc))(functools.partial(ffn_step, False))

    @pl.when(used & (f == nf - 1))
    def _():
        y_ref[...] = acc_ref[...]

    @pl.when(jnp.logical_not(used) & (f == nf - 1))
    def _():
        y_ref[...] = jnp.zeros_like(y_ref)


def _combine_body(p0_ref, p1_ref, np0_ref, np1_ref, y_hbm, rout_ref, x1_ref, g2_ref, pg_ref, o_ref,
                  yb_ref, sem, *, tc, n_steps):
    i = pl.program_id(0)
    slot = lax.rem(i, 2)
    d = o_ref.shape[-1]

    def start_rows(idx_refs, dst_slot):
        def body(it, carry):
            for j in range(8):
                for k in range(2):
                    p = idx_refs[k][0, 0, it * 8 + j]
                    pltpu.make_async_copy(y_hbm.at[pl.ds(p, 1)],
                                          yb_ref.at[dst_slot, k, it, pl.ds(j, 1)],
                                          sem.at[dst_slot]).start()
            return carry
        lax.fori_loop(0, tc // 8, body, 0)

    @pl.when(i == 0)
    def _():
        start_rows((p0_ref, p1_ref), 0)

    @pl.when(i + 1 < n_steps)
    def _():
        start_rows((np0_ref, np1_ref), 1 - slot)

    _wait_slot(yb_ref.at[slot], sem.at[slot])
    w = rout_ref[...]
    y = (w[:, 2:3] * yb_ref[slot, 0].reshape(tc, d) + w[:, 3:4] * yb_ref[slot, 1].reshape(tc, d))
    o_ref[...] = x1_ref[...] + g2_ref[0] * (y * _rms(y, d) * pg_ref[...])


def _moe_ffn(h2, rout, ridx, wg, wu, wd, x1, post_g, mod_l, seq):
    t, d = x1.shape
    f_dim = wg.shape[2]
    tm = min(512, seq)
    tf = _ffn_tile(f_dim)
    nf = f_dim // tf
    n_pairs = 2 * t
    n_tiles = n_pairs // tm + N_EXPERTS + 1
    n_rows = n_tiles * tm
    nc = min(nf, 8)
    cs = -(-tm // (nc * 16)) * 16
    tg = nc * cs

    expert = ridx[:2].astype(jnp.int32).reshape(1, n_pairs)
    onehot = (expert == jnp.arange(N_EXPERTS, dtype=jnp.int32)[:, None]).astype(jnp.int32)
    blocks = n_pairs // LANES
    tri = (jnp.arange(LANES)[:, None] <= jnp.arange(LANES)[None, :]).astype(F32)
    within = jnp.dot(onehot.reshape(N_EXPERTS * blocks, LANES).astype(F32), tri,
                     preferred_element_type=F32).astype(jnp.int32).reshape(N_EXPERTS, blocks, LANES)
    block_tot = within[:, :, -1]
    csum = (within + (jnp.cumsum(block_tot, axis=1) - block_tot)[:, :, None]).reshape(N_EXPERTS, n_pairs)
    rank = jnp.sum(onehot * csum, axis=0) - 1
    tiles_per = (csum[:, -1] + tm - 1) // tm
    tile_end = jnp.cumsum(tiles_per)
    tile_start = tile_end - tiles_per
    n_used = tile_end[-1]
    pos = jnp.sum(onehot * tile_start[:, None], axis=0) * tm + rank
    token = jnp.tile(jnp.arange(t, dtype=jnp.int32), 2)
    rows = jnp.zeros((n_rows,), jnp.int32).at[pos].set(token)
    tile_ids = jnp.minimum(jnp.arange(n_tiles, dtype=jnp.int32), n_used - 1)
    tile_expert = jnp.sum((tile_ids[:, None] >= tile_end[None, :]).astype(jnp.int32), axis=1)

    rows = jnp.pad(rows.reshape(n_tiles, tm), ((0, 0), (0, tg - tm))).reshape(n_tiles, 1, tg)

    y_sorted = pl.pallas_call(
        functools.partial(_moe_body, tm=tm, tf=tf, cs=cs, nc=nc, nf=nf),
        grid_spec=pltpu.PrefetchScalarGridSpec(
            num_scalar_prefetch=2,
            grid=(n_tiles, nf),
            in_specs=[pl.BlockSpec((1, 1, tg), lambda i, f, te, nu: (i, 0, 0),
                                   memory_space=pltpu.SMEM),
                      pl.BlockSpec((1, 1, tg),
                                   lambda i, f, te, nu: (jnp.minimum(i + 1, n_tiles - 1), 0, 0),
                                   memory_space=pltpu.SMEM),
                      pl.BlockSpec(memory_space=pl.ANY),
                      pl.BlockSpec(memory_space=pl.ANY),
                      pl.BlockSpec(memory_space=pl.ANY),
                      pl.BlockSpec(memory_space=pl.ANY)],
            out_specs=pl.BlockSpec((tm, d), lambda i, f, te, nu: (i, 0)),
            scratch_shapes=[pltpu.VMEM((2, nc, cs, d), F32), pltpu.VMEM((tm, d), BF16),
                            pltpu.VMEM((tm, d), F32), pltpu.SemaphoreType.DMA((2,))]
            + _weight_ring_scratch(d, tf)),
        out_shape=jax.ShapeDtypeStruct((n_rows, d), F32),
        compiler_params=_params("arbitrary", "arbitrary"),
        name="expert_ffn",
    )(tile_expert, n_used.reshape(1).astype(jnp.int32), rows, rows, h2, wg, wu, wd)

    tc = min(256, seq)
    per_seq = seq // tc
    n_steps = t // tc
    row = lambda i: (i, 0)
    pos = pos.reshape(2, n_steps, 1, tc)
    cur = pl.BlockSpec((1, 1, tc), lambda i: (i, 0, 0), memory_space=pltpu.SMEM)
    nxt = pl.BlockSpec((1, 1, tc), lambda i: (jnp.minimum(i + 1, n_steps - 1), 0, 0),
                       memory_space=pltpu.SMEM)
    return pl.pallas_call(
        functools.partial(_combine_body, tc=tc, n_steps=n_steps),
        grid=(n_steps,),
        in_specs=[cur, cur, nxt, nxt,
                  pl.BlockSpec(memory_space=pl.ANY),
                  pl.BlockSpec((tc, LANES), row),
                  pl.BlockSpec((tc, d), row),
                  pl.BlockSpec((1, 1, d), lambda i: ((i // per_seq) * 6 + 5, 0, 0)),
                  pl.BlockSpec((1, d), lambda i: (0, 0))],
        out_specs=pl.BlockSpec((tc, d), row),
        out_shape=jax.ShapeDtypeStruct((t, d), F32),
        scratch_shapes=[pltpu.VMEM((2, 2, tc // 8, 8, d), F32), pltpu.SemaphoreType.DMA((2,))],
        compiler_params=_params("arbitrary"),
        name="expert_combine",
    )(pos[0], pos[1], pos[0], pos[1], y_sorted, rout, x1, mod_l, post_g.reshape(1, d))


def kernel(x, c, w_mod, b_mod, pre_mix_g, post_mix_g, pre_ffn_g, post_ffn_g, w_in, q_norm_g, k_norm_g, sinks, grp_norm_a, grp_norm_b, w_out, w_ffn_gate, w_ffn_up, w_ffn_down, w_router, w_exp_gate, w_exp_up, w_exp_down):
    batch, seq, d = x.shape
    n_layers = w_mod.shape[0]
    assert seq % 256 == 0 or seq < 256, seq
    assert w_in.shape[2] == IN_COLS and w_out.shape[1] == 2 * Q_W

    mod = _modulation(c, w_mod, b_mod)
    tables = _rope_tables(seq)
    xf = x.reshape(batch * seq, d)
    expert_w = None
    for l in range(n_layers):
        mod_l = mod[l].reshape(batch * 6, 1, d)
        qkv = _qkv(xf, mod_l, pre_mix_g[l], w_in[l].astype(BF16), q_norm_g[l], k_norm_g[l],
                   tables, seq)
        merged = _attention(qkv, sinks[l], grp_norm_a[l], grp_norm_b[l], batch, seq)
        i = l // 2
        if l % 2 == 0:
            x1, h2 = _outproj(merged, w_out[l].astype(BF16), xf, post_mix_g[l], pre_ffn_g[l],
                              mod_l, seq)
            to_cast = ()
            if l + 1 < n_layers:
                to_cast = [w.reshape(-1, w.shape[-1])
                           for w in (w_exp_gate[i], w_exp_up[i], w_exp_down[i])]
            xf, cast = _dense_ffn(h2, w_ffn_gate[i].astype(BF16), w_ffn_up[i].astype(BF16),
                                  w_ffn_down[i].astype(BF16), x1, post_ffn_g[l], mod_l, seq,
                                  to_cast=to_cast)
            if cast:
                expert_w = [c2.reshape(w.shape[1:]) for c2, w in
                            zip(cast, (w_exp_gate, w_exp_up, w_exp_down))]
        else:
            x1, h2, rout, ridx = _outproj(merged, w_out[l].astype(BF16), xf, post_mix_g[l],
                                    pre_ffn_g[l], mod_l, seq, w_router=w_router[i])
            if expert_w is None:
                expert_w = [w[i].astype(BF16) for w in (w_exp_gate, w_exp_up, w_exp_down)]
            xf = _moe_ffn(h2, rout, ridx, *expert_w, x1, post_ffn_g[l], mod_l, seq)
            expert_w = None
    return xf.reshape(batch, seq, d)
```

```python
import functools

import jax
import jax.numpy as jnp
from jax import lax
from jax.experimental import pallas as pl
from jax.experimental.pallas import tpu as pltpu

F32 = jnp.float32
BF16 = jnp.bfloat16

HEAD_DIM = 128
N_Q = 8
N_KV = 2
GQA = N_Q // N_KV
Q_W = N_Q * HEAD_DIM
KV_W = N_KV * HEAD_DIM
IN_COLS = 2 * (Q_W + 2 * KV_W)
WINDOW = 128
GRID_W = 64
ROPE_THETA = 10000.0
N_EXPERTS = 8
EPS = 1e-6
LOG2_E = 1.4426950408889634
HEADS_PER_JOB_A = 2
HEADS_PER_JOB_B = 4
OUTPROJ_ROWS = 256
DENSE_RING = 3
EXPERT_RING = 4
JOBS_AHEAD = 2
LANES = 128
V7X_VMEM_LIMIT = 56 * 1024 * 1024


def _params(*sem):
    return pltpu.CompilerParams(dimension_semantics=sem, vmem_limit_bytes=V7X_VMEM_LIMIT)


def _resident(shape, index_map):
    return pl.BlockSpec(shape, index_map, pipeline_mode=pl.Buffered(1))


def _rms(v, width):
    return lax.rsqrt(jnp.sum(v * v, axis=-1, keepdims=True) * (1.0 / width) + EPS)


def _silu(v):
    return v * jax.nn.sigmoid(v)


def _mod_body(c_ref, w_ref, b_ref, o_ref):
    ca = _silu(c_ref[...]).astype(BF16)
    o_ref[0] = jnp.dot(ca, w_ref[0].astype(BF16), preferred_element_type=F32) + b_ref[0]


def _modulation(c, w_mod, b_mod):
    n_layers, d, n = w_mod.shape
    b = c.shape[0]
    tn = next(c for c in (1024, 512, 256, LANES) if n % c == 0)
    return pl.pallas_call(
        _mod_body,
        grid=(n_layers, n // tn),
        in_specs=[pl.BlockSpec((b, d), lambda l, j: (0, 0)),
                  pl.BlockSpec((1, d, tn), lambda l, j: (l, 0, j)),
                  pl.BlockSpec((1, 1, tn), lambda l, j: (l, 0, j))],
        out_specs=pl.BlockSpec((1, b, tn), lambda l, j: (l, 0, j)),
        out_shape=jax.ShapeDtypeStruct((n_layers, b, n), F32),
        compiler_params=_params("arbitrary", "arbitrary"),
        name="modulation",
    )(c, w_mod, b_mod.reshape(n_layers, 1, n))


def _rope_tables(seq):
    def angles(pos, dim):
        inv = 1.0 / (ROPE_THETA ** (jnp.arange(0, dim, 2, dtype=F32) / dim))
        return pos.astype(F32)[:, None] * inv[None, :]
    pos = jnp.arange(seq)
    a_row = angles(pos // GRID_W, HEAD_DIM // 2)
    a_col = angles(pos % GRID_W, HEAD_DIM // 2)
    a_seq = angles(pos, HEAD_DIM)
    cos_a = jnp.concatenate([jnp.cos(a_row)] * 2 + [jnp.cos(a_col)] * 2, axis=-1)
    sin_a = jnp.concatenate([-jnp.sin(a_row), jnp.sin(a_row), -jnp.sin(a_col), jnp.sin(a_col)], axis=-1)
    cos_b = jnp.concatenate([jnp.cos(a_seq)] * 2, axis=-1)
    sin_b = jnp.concatenate([-jnp.sin(a_seq), jnp.sin(a_seq)], axis=-1)
    return cos_a, sin_a, cos_b, sin_b


def _head_plan():
    plan = []
    for s in range(IN_COLS // HEAD_DIM):
        if s < 8:
            plan.append(("qa", s))
        elif s < 10:
            plan.append(("ka", s + 8))
        elif s < 12:
            plan.append(("v", s + 8))
        elif s < 20:
            plan.append(("qb", s - 4))
        elif s < 22:
            plan.append(("kb", s))
        else:
            plan.append(("v", s))
    return plan


def _qkv_body(x_ref, g_ref, sh_ref, sc_ref, w_ref, qg_ref, kg_ref,
              ca_ref, sa_ref, cb_ref, sb_ref, o_ref):
    x = x_ref[...]
    d = x.shape[-1]
    xn = x * _rms(x, d) * g_ref[...]
    h = (xn * (1.0 + sc_ref[0]) + sh_ref[0]).astype(BF16)

    tm = x.shape[0]
    lane = lax.broadcasted_iota(jnp.int32, (tm, HEAD_DIM), 1)
    first_quarter = (lane & (HEAD_DIM // 2 - 1)) < (HEAD_DIM // 4)
    ca, sa, cb, sb = ca_ref[...], sa_ref[...], cb_ref[...], sb_ref[...]
    scale = HEAD_DIM ** -0.5 * LOG2_E

    def norm(a, g):
        return a * _rms(a, HEAD_DIM) * g

    def rope_a(a):
        partner = jnp.where(first_quarter, pltpu.roll(a, 96, 1), pltpu.roll(a, 32, 1))
        return a * ca + partner * sa

    def rope_b(a):
        return a * cb + pltpu.roll(a, 64, 1) * sb

    plan = _head_plan()
    chunk = 2 * HEAD_DIM
    for c in range(IN_COLS // chunk):
        acc = jnp.dot(h, w_ref[:, c * chunk:(c + 1) * chunk], preferred_element_type=F32)
        for half in range(2):
            kind, dst = plan[2 * c + half]
            a = acc[:, half * HEAD_DIM:(half + 1) * HEAD_DIM]
            if kind == "qa":
                a = rope_a(norm(a, qg_ref[...])) * scale
            elif kind == "ka":
                a = rope_a(norm(a, kg_ref[...]))
            elif kind == "qb":
                a = rope_b(a) * scale
            elif kind == "kb":
                a = rope_b(a)
            o_ref[:, dst * HEAD_DIM:(dst + 1) * HEAD_DIM] = a.astype(o_ref.dtype)


def _qkv(xf, mod_l, pre_g, w_in, qg, kg, tables, seq):
    t, d = xf.shape
    tm = min(512, seq)
    per_seq = seq // tm
    row = lambda i: (i, 0)
    tab = lambda i: (i % per_seq, 0)
    vec = pl.BlockSpec((1, d), lambda i: (0, 0))
    hvec = pl.BlockSpec((1, HEAD_DIM), lambda i: (0, 0))
    tspec = pl.BlockSpec((tm, HEAD_DIM), tab)
    return pl.pallas_call(
        _qkv_body,
        grid=(t // tm,),
        in_specs=[pl.BlockSpec((tm, d), row), vec,
                  pl.BlockSpec((1, 1, d), lambda i: ((i // per_seq) * 6 + 0, 0, 0)),
                  pl.BlockSpec((1, 1, d), lambda i: ((i // per_seq) * 6 + 1, 0, 0)),
                  _resident((d, IN_COLS), lambda i: (0, 0)),
                  hvec, hvec, tspec, tspec, tspec, tspec],
        out_specs=pl.BlockSpec((tm, IN_COLS), row),
        out_shape=jax.ShapeDtypeStruct((t, IN_COLS), BF16),
        compiler_params=_params("arbitrary"),
        name="qkv_proj",
    )(xf, pre_g.reshape(1, d), mod_l, mod_l, w_in, qg.reshape(1, HEAD_DIM), kg.reshape(1, HEAD_DIM),
      *tables)


ONES_ROWS = 16


def _values_t(v):
    vt = v.astype(F32).T.astype(BF16)
    return jnp.concatenate([vt, jnp.ones((ONES_ROWS, vt.shape[1]), BF16)], axis=0)


def _reduce_rows(fn, v):
    rows, n = v.shape
    if rows % 128 == 0 and rows > 128:
        v = fn(v.reshape(rows // 128, 128, n), axis=0)
    return fn(v, axis=0, keepdims=True)


def _attn_body(sinks_ref, qa_ref, qb_ref, kva_ref, kvb_ref, gna_ref, gnb_ref, o_ref,
               vt_ref, acc_ref, *, tq, seq, wk):
    i = pl.program_id(1)
    nt = (((1,), (1,)), ((), ()))

    @pl.when(i == 0)
    def _():
        for kv in range(N_KV):
            vt_ref[kv] = _values_t(kva_ref[:, KV_W + kv * HEAD_DIM:KV_W + (kv + 1) * HEAD_DIM])

    def stack_q(q_ref, hd0, n):
        qs = [q_ref[:, hd * HEAD_DIM:(hd + 1) * HEAD_DIM] for hd in range(hd0, hd0 + n)]
        return qs[0] if n == 1 else jnp.concatenate(qs, axis=0)

    def scores_a(hd0, n):
        k = kva_ref[:, (hd0 // GQA) * HEAD_DIM:(hd0 // GQA + 1) * HEAD_DIM]
        return lax.dot_general(k, stack_q(qa_ref, hd0, n), nt, preferred_element_type=F32)

    def finish_a(hd0, n, s):
        m = _reduce_rows(jnp.max, s)
        p = jnp.exp2(s - m).astype(BF16)
        o = jnp.dot(vt_ref[hd0 // GQA], p, preferred_element_type=F32)
        o = o[:HEAD_DIM] / o[HEAD_DIM:HEAD_DIM + 1]
        for j in range(n):
            acc_ref[hd0 + j] = o[:, j * tq:(j + 1) * tq]

    start = pl.multiple_of(jnp.clip(i * tq - WINDOW, 0, seq - wk), WINDOW)
    kpos = start + lax.broadcasted_iota(jnp.int32, (wk, tq), 0)
    qpos = i * tq + lax.broadcasted_iota(jnp.int32, (wk, tq), 1)
    bias = jnp.where(jnp.abs(kpos - qpos) <= WINDOW, 0.0, -jnp.inf)
    vt_b = [_values_t(kvb_ref[pl.ds(start, wk), KV_W + kv * HEAD_DIM:KV_W + (kv + 1) * HEAD_DIM])
            for kv in range(N_KV)]

    def scores_b(hd0, n):
        k = kvb_ref[pl.ds(start, wk), (hd0 // GQA) * HEAD_DIM:(hd0 // GQA + 1) * HEAD_DIM]
        s = lax.dot_general(k, stack_q(qb_ref, hd0, n), nt, preferred_element_type=F32)
        return s + (bias if n == 1 else jnp.concatenate([bias] * n, axis=1))

    def finish_b(hd0, n, s):
        sinks = [jnp.full((1, tq), sinks_ref[hd0 + j] * LOG2_E, F32) for j in range(n)]
        sink = sinks[0] if n == 1 else jnp.concatenate(sinks, axis=1)
        m = jnp.maximum(_reduce_rows(jnp.max, s), sink)
        e = jnp.exp2(s - m).astype(BF16)
        o = jnp.dot(vt_b[hd0 // GQA], e, preferred_element_type=F32)
        o = o[:HEAD_DIM] / (o[HEAD_DIM:HEAD_DIM + 1] + jnp.exp2(sink - m))
        for j in range(n):
            acc_ref[N_Q + hd0 + j] = o[:, j * tq:(j + 1) * tq]

    jobs = ([(scores_a, finish_a, hd0, HEADS_PER_JOB_A) for hd0 in range(0, N_Q, HEADS_PER_JOB_A)]
            + [(scores_b, finish_b, hd0, HEADS_PER_JOB_B) for hd0 in range(0, N_Q, HEADS_PER_JOB_B)])
    ahead = min(JOBS_AHEAD, len(jobs))
    pending = [jobs[j][0](jobs[j][2], jobs[j][3]) for j in range(ahead)]
    for j, (_, finish, hd0, n) in enumerate(jobs):
        if j + ahead < len(jobs):
            nxt = jobs[j + ahead]
            pending.append(nxt[0](nxt[2], nxt[3]))
        finish(hd0, n, pending.pop(0))

    for grp, gain_ref in enumerate((gna_ref, gnb_ref)):
        ss = jnp.zeros((1, tq), F32)
        for hd in range(N_Q):
            o = acc_ref[grp * N_Q + hd]
            ss = ss + jnp.sum(o * o, axis=0, keepdims=True)
        r = lax.rsqrt(ss * (1.0 / Q_W) + EPS)
        for hd in range(N_Q):
            col = grp * Q_W + hd * HEAD_DIM
            o = acc_ref[grp * N_Q + hd] * r * gain_ref[hd]
            o_ref[:, col:col + HEAD_DIM] = o.astype(o_ref.dtype).T


def _attention(qkv, sinks, gna, gnb, batch, seq):
    t = qkv.shape[0]
    tq = min(256, seq)
    wk = min(tq + 2 * WINDOW, seq)
    per_seq = seq // tq
    row = lambda b, i: (b * per_seq + i, 0)
    body = functools.partial(_attn_body, tq=tq, seq=seq, wk=wk)
    gvec = pl.BlockSpec((N_Q, HEAD_DIM, 1), lambda b, i: (0, 0, 0))
    return pl.pallas_call(
        body,
        grid=(batch, per_seq),
        in_specs=[pl.BlockSpec(memory_space=pltpu.SMEM),
                  pl.BlockSpec((tq, Q_W), lambda b, i: (b * per_seq + i, 0)),
                  pl.BlockSpec((tq, Q_W), lambda b, i: (b * per_seq + i, 1)),
                  pl.BlockSpec((seq, 2 * KV_W), lambda b, i: (b, 4)),
                  pl.BlockSpec((seq, 2 * KV_W), lambda b, i: (b, 5)),
                  gvec, gvec],
        out_specs=pl.BlockSpec((tq, 2 * Q_W), row),
        out_shape=jax.ShapeDtypeStruct((t, 2 * Q_W), BF16),
        scratch_shapes=[pltpu.VMEM((N_KV, HEAD_DIM + ONES_ROWS, seq), BF16),
                        pltpu.VMEM((2 * N_Q, HEAD_DIM, tq), F32)],
        compiler_params=_params("arbitrary", "arbitrary"),
        name="attention",
    )(sinks, qkv, qkv, qkv, qkv, gna.reshape(N_Q, HEAD_DIM, 1), gnb.reshape(N_Q, HEAD_DIM, 1))


def _outproj_body(*refs, router):
    if router:
        (m_ref, w_ref, x_ref, pg_ref, g1_ref, fg_ref, sh2_ref, sc2_ref, wr_ref,
         x1_ref, h2_ref, rout_ref, ridx_ref) = refs
    else:
        m_ref, w_ref, x_ref, pg_ref, g1_ref, fg_ref, sh2_ref, sc2_ref, x1_ref, h2_ref = refs
    tm, d = x_ref.shape
    rows = min(tm, OUTPROJ_ROWS)

    def project(r):
        return jnp.dot(m_ref[r * rows:(r + 1) * rows, :], w_ref[...], preferred_element_type=F32)

    def epilogue(r, y):
        sl = slice(r * rows, (r + 1) * rows)
        x1 = x_ref[sl, :] + g1_ref[0] * (y * _rms(y, d) * pg_ref[...])
        x1_ref[sl, :] = x1
        h2 = (x1 * _rms(x1, d) * fg_ref[...]) * (1.0 + sc2_ref[0]) + sh2_ref[0]
        h2_ref[sl, :] = h2.astype(h2_ref.dtype)
        if router:
            logits = jnp.dot(h2.astype(BF16), wr_ref[...], preferred_element_type=F32)
            lane = lax.broadcasted_iota(jnp.int32, logits.shape, 1).astype(F32)
            lg = jnp.where(lane < N_EXPERTS, logits, -jnp.inf)
            m1 = jnp.max(lg, axis=-1, keepdims=True)
            i1 = jnp.min(jnp.where(lg == m1, lane, float(LANES)), axis=-1, keepdims=True)
            lg2 = jnp.where(lane == i1, -jnp.inf, lg)
            m2 = jnp.max(lg2, axis=-1, keepdims=True)
            i2 = jnp.min(jnp.where(lg2 == m2, lane, float(LANES)), axis=-1, keepdims=True)
            e2 = jnp.exp(m2 - m1)
            den = 1.0 + e2
            rout = jnp.where(lane == 0, i1, jnp.where(lane == 1, i2, jnp.where(
                lane == 2, 1.0 / den, jnp.where(lane == 3, e2 / den, 0.0))))
            rout_ref[sl, :] = rout
            ridx_ref[:, sl] = rout.T[:8, :]

    y = project(0)
    for r in range(tm // rows):
        y_next = project(r + 1) if r + 1 < tm // rows else None
        epilogue(r, y)
        y = y_next


def _outproj(merged, w_out, xf, post_g, ffn_g, mod_l, seq, w_router=None):
    t, d = xf.shape
    k = merged.shape[1]
    tm = min(512, seq)
    per_seq = seq // tm
    row = lambda i: (i, 0)
    vec = pl.BlockSpec((1, d), lambda i: (0, 0))
    modv = lambda j: pl.BlockSpec((1, 1, d), lambda i: ((i // per_seq) * 6 + j, 0, 0))
    router = w_router is not None
    in_specs = [pl.BlockSpec((tm, k), row), _resident((k, d), lambda i: (0, 0)),
                pl.BlockSpec((tm, d), row), vec, modv(2), vec, modv(3), modv(4)]
    args = [merged, w_out, xf, post_g.reshape(1, d), mod_l, ffn_g.reshape(1, d), mod_l, mod_l]
    out_specs = [pl.BlockSpec((tm, d), row), pl.BlockSpec((tm, d), row)]
    out_shape = [jax.ShapeDtypeStruct((t, d), F32),
                 jax.ShapeDtypeStruct((t, d), F32 if router else BF16)]
    if router:
        wr = jnp.zeros((d, LANES), BF16).at[:, :N_EXPERTS].set(w_router.astype(BF16))
        in_specs.append(_resident((d, LANES), lambda i: (0, 0)))
        args.append(wr)
        out_specs += [pl.BlockSpec((tm, LANES), row), pl.BlockSpec((8, tm), lambda i: (0, i))]
        out_shape += [jax.ShapeDtypeStruct((t, LANES), F32), jax.ShapeDtypeStruct((8, t), F32)]
    return pl.pallas_call(
        functools.partial(_outproj_body, router=router),
        grid=(t // tm,),
        in_specs=in_specs, out_specs=out_specs, out_shape=out_shape,
        compiler_params=_params("arbitrary"),
        name="out_proj_router" if router else "out_proj",
    )(*args)


def _ring_copies(srcs, bufs, sems, slot):
    return [pltpu.make_async_copy(src, buf.at[slot], sems.at[k, slot])
            for k, (src, buf) in enumerate(zip(srcs, bufs))]


def _ffn_body(h_ref, wg_hbm, wu_hbm, wd_hbm, x1_ref, g2_ref, pg_ref, *rest, nf, n_steps, tf, n_cast,
              ring=DENSE_RING):
    cast_in, o_ref = rest[:n_cast], rest[n_cast]
    cast_out = rest[n_cast + 1:2 * n_cast + 1]
    wg_buf, wu_buf, wd_buf, wsem = rest[2 * n_cast + 1:]
    f = pl.program_id(1)
    step = pl.program_id(0) * nf + f
    slot = lax.rem(step, ring)

    def chunk_copies(chunk, dst_slot):
        cols = pl.ds(pl.multiple_of(chunk * tf, tf), tf)
        return _ring_copies((wg_hbm.at[:, cols], wu_hbm.at[:, cols], wd_hbm.at[cols, :]),
                            (wg_buf, wu_buf, wd_buf), wsem, dst_slot)

    @pl.when(step == 0)
    def _():
        for k in range(ring - 1):
            for c in chunk_copies(k % nf, k):
                c.start()

    @pl.when(step + (ring - 1) < n_steps)
    def _():
        ahead = step + (ring - 1)
        for c in chunk_copies(lax.rem(ahead, nf), lax.rem(ahead, ring)):
            c.start()

    for c in chunk_copies(f, slot):
        c.wait()

    @pl.when(f == 0)
    def _():
        o_ref[...] = jnp.zeros_like(o_ref)

    for src, dst in zip(cast_in, cast_out):
        dst[...] = src[...].astype(dst.dtype)

    h = h_ref[...]
    g = jnp.dot(h, wg_buf[slot], preferred_element_type=F32)
    u = jnp.dot(h, wu_buf[slot], preferred_element_type=F32)
    a = (_silu(g) * u).astype(BF16)
    o_ref[...] += jnp.dot(a, wd_buf[slot], preferred_element_type=F32)

    @pl.when(f == nf - 1)
    def _():
        y = o_ref[...]
        o_ref[...] = x1_ref[...] + g2_ref[0] * (y * _rms(y, y.shape[-1]) * pg_ref[...])


def _weight_ring_scratch(d, tf, ring):
    return [pltpu.VMEM((ring, d, tf), BF16), pltpu.VMEM((ring, d, tf), BF16),
            pltpu.VMEM((ring, tf, d), BF16), pltpu.SemaphoreType.DMA((3, ring))]


def _ffn_tile(f_dim, prefer=(512, 256, 128)):
    for tf in prefer:
        if f_dim % tf == 0:
            return tf
    return f_dim


def _cast_spec(arr, n_i, nf):
    rows, cols = arr.shape
    if rows % n_i == 0 and cols % nf == 0 and (rows // n_i) % 16 == 0 and (cols // nf) % LANES == 0:
        return pl.BlockSpec((rows // n_i, cols // nf), lambda i, f: (i, f))
    if rows % (n_i * nf) == 0 and (rows // (n_i * nf)) % 16 == 0:
        return pl.BlockSpec((rows // (n_i * nf), cols), lambda i, f: (i * nf + f, 0))
    return None


def _dense_ffn(h2, wg, wu, wd, x1, post_g, mod_l, seq, to_cast=()):
    t, d = x1.shape
    f_dim = wg.shape[1]
    tm = min(512, seq)
    tf = _ffn_tile(f_dim)
    nf = f_dim // tf
    n_i = t // tm
    per_seq = seq // tm
    row = lambda i, f: (i, 0)
    cast_specs = [_cast_spec(a, t // tm, nf) for a in to_cast]
    hosted = [a for a, sp in zip(to_cast, cast_specs) if sp is not None]
    specs = [sp for sp in cast_specs if sp is not None]
    outs = pl.pallas_call(
        functools.partial(_ffn_body, nf=nf, n_steps=n_i * nf, tf=tf, n_cast=len(hosted)),
        grid=(n_i, nf),
        in_specs=[pl.BlockSpec((tm, d), row),
                  pl.BlockSpec(memory_space=pl.ANY),
                  pl.BlockSpec(memory_space=pl.ANY),
                  pl.BlockSpec(memory_space=pl.ANY),
                  pl.BlockSpec((tm, d), row),
                  pl.BlockSpec((1, 1, d), lambda i, f: ((i // per_seq) * 6 + 5, 0, 0)),
                  pl.BlockSpec((1, d), lambda i, f: (0, 0))] + specs,
        out_specs=[pl.BlockSpec((tm, d), row)] + specs,
        out_shape=[jax.ShapeDtypeStruct((t, d), F32)]
        + [jax.ShapeDtypeStruct(a.shape, BF16) for a in hosted],
        scratch_shapes=_weight_ring_scratch(d, tf, DENSE_RING),
        compiler_params=_params("arbitrary", "arbitrary"),
        name="dense_ffn",
    )(h2, wg, wu, wd, x1, mod_l, post_g.reshape(1, d), *hosted)
    done = iter(outs[1:])
    return outs[0], [next(done) if sp is not None else a.astype(BF16)
                     for a, sp in zip(to_cast, cast_specs)]


def _wait_slot(buf_ref, sem):
    pltpu.make_async_copy(buf_ref, buf_ref, sem).wait()


def _moe_body(te_ref, nu_ref, rows_ref, nrows_ref, h_hbm, wg_hbm, wu_hbm, wd_hbm, y_ref,
              xg_ref, hb_ref, sem, wg_buf, wu_buf, wd_buf, wsem, *, tm, tf, cs, nc, nf,
              ring=EXPERT_RING):
    i = pl.program_id(0)
    f = pl.program_id(1)
    n_used = nu_ref[0]
    used = i < n_used
    slot = lax.rem(i, 2)
    step = i * nf + f
    n_steps = n_used * nf
    wslot = lax.rem(step, ring)

    def chunk_copies(tile, chunk, dst_slot):
        e = te_ref[tile]
        cols = pl.ds(pl.multiple_of(chunk * tf, tf), tf)
        return _ring_copies((wg_hbm.at[e, :, cols], wu_hbm.at[e, :, cols], wd_hbm.at[e, cols, :]),
                            (wg_buf, wu_buf, wd_buf), wsem, dst_slot)

    @pl.when(step == 0)
    def _():
        for k in range(ring - 1):
            for c in chunk_copies(k // nf, k % nf, k):
                c.start()

    @pl.when(step + (ring - 1) < n_steps)
    def _():
        ahead = step + (ring - 1)
        tile = ahead // nf
        for c in chunk_copies(tile, ahead - tile * nf, lax.rem(ahead, ring)):
            c.start()

    @pl.when(used)
    def _():
        for c in chunk_copies(i, f, wslot):
            c.wait()

    def start_row(idx_ref, chunk, j, dst_slot):
        tok = idx_ref[0, 0, chunk * cs + j]
        pltpu.make_async_copy(h_hbm.at[pl.ds(tok, 1)], xg_ref.at[dst_slot, chunk, pl.ds(j, 1)],
                              sem.at[dst_slot]).start()

    @pl.when((i == 0) & (f == 0))
    def _():
        def chunk_body(c, carry):
            for j in range(cs):
                start_row(rows_ref, c, j, 0)
            return carry
        lax.fori_loop(0, nc, chunk_body, 0)

    @pl.when((i <= n_used) & (f == 0))
    def _():
        _wait_slot(xg_ref.at[slot], sem.at[slot])

    @pl.when(used & (f == 0))
    def _():
        for c in range(nc):
            n = min(cs, tm - c * cs)
            if n > 0:
                hb_ref[c * cs:c * cs + n, :] = xg_ref[slot, c, :n, :].astype(BF16)
        y_ref[...] = jnp.zeros_like(y_ref)

    def ffn_step(issue):
        if issue:
            for j in range(cs):
                start_row(nrows_ref, f, j, 1 - slot)
        h = hb_ref[...]
        g = jnp.dot(h, wg_buf[wslot], preferred_element_type=F32)
        u = jnp.dot(h, wu_buf[wslot], preferred_element_type=F32)
        a = (_silu(g) * u).astype(BF16)
        y_ref[...] += jnp.dot(a, wd_buf[wslot], preferred_element_type=F32)

    pl.when(used & (f < nc))(functools.partial(ffn_step, True))
    if nc < nf:
        pl.when(used & (f >= nc))(functools.partial(ffn_step, False))

    @pl.when(jnp.logical_not(used) & (f == nf - 1))
    def _():
        y_ref[...] = jnp.zeros_like(y_ref)


def _combine_body(p0_ref, p1_ref, np0_ref, np1_ref, y_hbm, rout_ref, x1_ref, g2_ref, pg_ref, o_ref,
                  yb_ref, sem, *, tc, n_steps):
    i = pl.program_id(0)
    slot = lax.rem(i, 2)
    d = o_ref.shape[-1]

    def start_rows(idx_refs, dst_slot):
        def body(it, carry):
            for j in range(8):
                for k in range(2):
                    p = idx_refs[k][0, 0, it * 8 + j]
                    pltpu.make_async_copy(y_hbm.at[pl.ds(p, 1)],
                                          yb_ref.at[dst_slot, k, it, pl.ds(j, 1)],
                                          sem.at[dst_slot]).start()
            return carry
        lax.fori_loop(0, tc // 8, body, 0)

    @pl.when(i == 0)
    def _():
        start_rows((p0_ref, p1_ref), 0)

    @pl.when(i + 1 < n_steps)
    def _():
        start_rows((np0_ref, np1_ref), 1 - slot)

    _wait_slot(yb_ref.at[slot], sem.at[slot])
    w = rout_ref[...]
    y = (w[:, 2:3] * yb_ref[slot, 0].reshape(tc, d) + w[:, 3:4] * yb_ref[slot, 1].reshape(tc, d))
    o_ref[...] = x1_ref[...] + g2_ref[0] * (y * _rms(y, d) * pg_ref[...])


def _moe_ffn(h2, rout, ridx, wg, wu, wd, x1, post_g, mod_l, seq):
    t, d = x1.shape
    f_dim = wg.shape[2]
    tm = min(512, seq)
    tf = _ffn_tile(f_dim)
    nf = f_dim // tf
    n_pairs = 2 * t
    n_tiles = n_pairs // tm + N_EXPERTS + 1
    n_rows = n_tiles * tm
    nc = min(nf, 8)
    cs = -(-tm // (nc * 16)) * 16
    tg = nc * cs

    expert = ridx[:2].astype(jnp.int32).reshape(1, n_pairs)
    onehot = (expert == jnp.arange(N_EXPERTS, dtype=jnp.int32)[:, None]).astype(jnp.int32)
    blocks = n_pairs // LANES
    tri = (jnp.arange(LANES)[:, None] <= jnp.arange(LANES)[None, :]).astype(F32)
    within = jnp.dot(onehot.reshape(N_EXPERTS * blocks, LANES).astype(F32), tri,
                     preferred_element_type=F32).astype(jnp.int32).reshape(N_EXPERTS, blocks, LANES)
    block_tot = within[:, :, -1]
    csum = (within + (jnp.cumsum(block_tot, axis=1) - block_tot)[:, :, None]).reshape(N_EXPERTS, n_pairs)
    rank = jnp.sum(onehot * csum, axis=0) - 1
    tiles_per = (csum[:, -1] + tm - 1) // tm
    tile_end = jnp.cumsum(tiles_per)
    tile_start = tile_end - tiles_per
    n_used = tile_end[-1]
    pos = jnp.sum(onehot * tile_start[:, None], axis=0) * tm + rank
    token = jnp.tile(jnp.arange(t, dtype=jnp.int32), 2)
    rows = jnp.zeros((n_rows,), jnp.int32).at[pos].set(token)
    tile_ids = jnp.minimum(jnp.arange(n_tiles, dtype=jnp.int32), n_used - 1)
    tile_expert = jnp.sum((tile_ids[:, None] >= tile_end[None, :]).astype(jnp.int32), axis=1)

    rows = jnp.pad(rows.reshape(n_tiles, tm), ((0, 0), (0, tg - tm))).reshape(n_tiles, 1, tg)

    y_sorted = pl.pallas_call(
        functools.partial(_moe_body, tm=tm, tf=tf, cs=cs, nc=nc, nf=nf),
        grid_spec=pltpu.PrefetchScalarGridSpec(
            num_scalar_prefetch=2,
            grid=(n_tiles, nf),
            in_specs=[pl.BlockSpec((1, 1, tg), lambda i, f, te, nu: (i, 0, 0),
                                   memory_space=pltpu.SMEM),
                      pl.BlockSpec((1, 1, tg),
                                   lambda i, f, te, nu: (jnp.minimum(i + 1, n_tiles - 1), 0, 0),
                                   memory_space=pltpu.SMEM),
                      pl.BlockSpec(memory_space=pl.ANY),
                      pl.BlockSpec(memory_space=pl.ANY),
                      pl.BlockSpec(memory_space=pl.ANY),
                      pl.BlockSpec(memory_space=pl.ANY)],
            out_specs=pl.BlockSpec((tm, d), lambda i, f, te, nu: (i, 0)),
            scratch_shapes=[pltpu.VMEM((2, nc, cs, d), F32), pltpu.VMEM((tm, d), BF16),
                            pltpu.SemaphoreType.DMA((2,))]
            + _weight_ring_scratch(d, tf, EXPERT_RING)),
        out_shape=jax.ShapeDtypeStruct((n_rows, d), F32),
        compiler_params=_params("arbitrary", "arbitrary"),
        name="expert_ffn",
    )(tile_expert, n_used.reshape(1).astype(jnp.int32), rows, rows, h2, wg, wu, wd)

    tc = min(256, seq)
    per_seq = seq // tc
    n_steps = t // tc
    row = lambda i: (i, 0)
    pos = pos.reshape(2, n_steps, 1, tc)
    cur = pl.BlockSpec((1, 1, tc), lambda i: (i, 0, 0), memory_space=pltpu.SMEM)
    nxt = pl.BlockSpec((1, 1, tc), lambda i: (jnp.minimum(i + 1, n_steps - 1), 0, 0),
                       memory_space=pltpu.SMEM)
    return pl.pallas_call(
        functools.partial(_combine_body, tc=tc, n_steps=n_steps),
        grid=(n_steps,),
        in_specs=[cur, cur, nxt, nxt,
                  pl.BlockSpec(memory_space=pl.ANY),
                  pl.BlockSpec((tc, LANES), row),
                  pl.BlockSpec((tc, d), row),
                  pl.BlockSpec((1, 1, d), lambda i: ((i // per_seq) * 6 + 5, 0, 0)),
                  pl.BlockSpec((1, d), lambda i: (0, 0))],
        out_specs=pl.BlockSpec((tc, d), row),
        out_shape=jax.ShapeDtypeStruct((t, d), F32),
        scratch_shapes=[pltpu.VMEM((2, 2, tc // 8, 8, d), F32), pltpu.SemaphoreType.DMA((2,))],
        compiler_params=_params("arbitrary"),
        name="expert_combine",
    )(pos[0], pos[1], pos[0], pos[1], y_sorted, rout, x1, mod_l, post_g.reshape(1, d))


def kernel(x, c, w_mod, b_mod, pre_mix_g, post_mix_g, pre_ffn_g, post_ffn_g, w_in, q_norm_g, k_norm_g, sinks, grp_norm_a, grp_norm_b, w_out, w_ffn_gate, w_ffn_up, w_ffn_down, w_router, w_exp_gate, w_exp_up, w_exp_down):
    batch, seq, d = x.shape
    n_layers = w_mod.shape[0]
    assert seq % 256 == 0 or seq < 256, seq
    assert w_in.shape[2] == IN_COLS and w_out.shape[1] == 2 * Q_W

    mod = _modulation(c, w_mod, b_mod)
    tables = _rope_tables(seq)
    xf = x.reshape(batch * seq, d)
    expert_w = None
    for l in range(n_layers):
        mod_l = mod[l].reshape(batch * 6, 1, d)
        qkv = _qkv(xf, mod_l, pre_mix_g[l], w_in[l].astype(BF16), q_norm_g[l], k_norm_g[l],
                   tables, seq)
        merged = _attention(qkv, sinks[l], grp_norm_a[l], grp_norm_b[l], batch, seq)
        i = l // 2
        if l % 2 == 0:
            x1, h2 = _outproj(merged, w_out[l].astype(BF16), xf, post_mix_g[l], pre_ffn_g[l],
                              mod_l, seq)
            to_cast = ()
            if l + 1 < n_layers:
                to_cast = [w.reshape(-1, w.shape[-1])
                           for w in (w_exp_gate[i], w_exp_up[i], w_exp_down[i])]
            xf, cast = _dense_ffn(h2, w_ffn_gate[i].astype(BF16), w_ffn_up[i].astype(BF16),
                                  w_ffn_down[i].astype(BF16), x1, post_ffn_g[l], mod_l, seq,
                                  to_cast=to_cast)
            if cast:
                expert_w = [c2.reshape(w.shape[1:]) for c2, w in
                            zip(cast, (w_exp_gate, w_exp_up, w_exp_down))]
        else:
            x1, h2, rout, ridx = _outproj(merged, w_out[l].astype(BF16), xf, post_mix_g[l],
                                    pre_ffn_g[l], mod_l, seq, w_router=w_router[i])
            if expert_w is None:
                expert_w = [w[i].astype(BF16) for w in (w_exp_gate, w_exp_up, w_exp_down)]
            xf = _moe_ffn(h2, rout, ridx, *expert_w, x1, post_ffn_g[l], mod_l, seq)
            expert_w = None
    return xf.reshape(batch, seq, d)
```

```python
import functools

import jax
import jax.numpy as jnp
from jax import lax
from jax.experimental import pallas as pl
from jax.experimental.pallas import tpu as pltpu

F32 = jnp.float32
BF16 = jnp.bfloat16

HEAD_DIM = 128
N_Q = 8
N_KV = 2
GQA = N_Q // N_KV
Q_W = N_Q * HEAD_DIM
KV_W = N_KV * HEAD_DIM
IN_COLS = 2 * (Q_W + 2 * KV_W)
WINDOW = 128
GRID_W = 64
ROPE_THETA = 10000.0
N_EXPERTS = 8
EPS = 1e-6
LOG2_E = 1.4426950408889634
HEADS_PER_JOB_A = 2
HEADS_PER_JOB_B = 4
OUTPROJ_ROWS = 256
DENSE_RING = 3
EXPERT_RING = 4
JOBS_AHEAD = 2
LANES = 128
V7X_VMEM_LIMIT = 56 * 1024 * 1024


def _params(*sem):
    return pltpu.CompilerParams(dimension_semantics=sem, vmem_limit_bytes=V7X_VMEM_LIMIT)


def _resident(shape, index_map):
    return pl.BlockSpec(shape, index_map, pipeline_mode=pl.Buffered(1))


def _rms(v, width):
    return lax.rsqrt(jnp.sum(v * v, axis=-1, keepdims=True) * (1.0 / width) + EPS)


def _silu(v):
    return v * jax.nn.sigmoid(v)


def _mod_body(c_ref, w_ref, b_ref, o_ref):
    ca = _silu(c_ref[...]).astype(BF16)
    o_ref[0] = jnp.dot(ca, w_ref[0].astype(BF16), preferred_element_type=F32) + b_ref[0]


def _modulation(c, w_mod, b_mod):
    n_layers, d, n = w_mod.shape
    b = c.shape[0]
    tn = next(c for c in (1024, 512, 256, LANES) if n % c == 0)
    return pl.pallas_call(
        _mod_body,
        grid=(n_layers, n // tn),
        in_specs=[pl.BlockSpec((b, d), lambda l, j: (0, 0)),
                  pl.BlockSpec((1, d, tn), lambda l, j: (l, 0, j)),
                  pl.BlockSpec((1, 1, tn), lambda l, j: (l, 0, j))],
        out_specs=pl.BlockSpec((1, b, tn), lambda l, j: (l, 0, j)),
        out_shape=jax.ShapeDtypeStruct((n_layers, b, n), F32),
        compiler_params=_params("arbitrary", "arbitrary"),
        name="modulation",
    )(c, w_mod, b_mod.reshape(n_layers, 1, n))


def _rope_tables(seq):
    def angles(pos, dim):
        inv = 1.0 / (ROPE_THETA ** (jnp.arange(0, dim, 2, dtype=F32) / dim))
        return pos.astype(F32)[:, None] * inv[None, :]
    pos = jnp.arange(seq)
    a_row = angles(pos // GRID_W, HEAD_DIM // 2)
    a_col = angles(pos % GRID_W, HEAD_DIM // 2)
    a_seq = angles(pos, HEAD_DIM)
    cos_a = jnp.concatenate([jnp.cos(a_row)] * 2 + [jnp.cos(a_col)] * 2, axis=-1)
    sin_a = jnp.concatenate([-jnp.sin(a_row), jnp.sin(a_row), -jnp.sin(a_col), jnp.sin(a_col)], axis=-1)
    cos_b = jnp.concatenate([jnp.cos(a_seq)] * 2, axis=-1)
    sin_b = jnp.concatenate([-jnp.sin(a_seq), jnp.sin(a_seq)], axis=-1)
    return cos_a, sin_a, cos_b, sin_b


def _head_plan():
    plan = []
    for s in range(IN_COLS // HEAD_DIM):
        if s < 8:
            plan.append(("qa", s))
        elif s < 10:
            plan.append(("ka", s + 8))
        elif s < 12:
            plan.append(("v", s + 8))
        elif s < 20:
            plan.append(("qb", s - 4))
        elif s < 22:
            plan.append(("kb", s))
        else:
            plan.append(("v", s))
    return plan


def _qkv_body(x_ref, g_ref, sh_ref, sc_ref, w_ref, qg_ref, kg_ref,
              ca_ref, sa_ref, cb_ref, sb_ref, o_ref):
    x = x_ref[...]
    d = x.shape[-1]
    xn = x * _rms(x, d) * g_ref[...]
    h = (xn * (1.0 + sc_ref[0]) + sh_ref[0]).astype(BF16)

    tm = x.shape[0]
    lane = lax.broadcasted_iota(jnp.int32, (tm, HEAD_DIM), 1)
    first_quarter = (lane & (HEAD_DIM // 2 - 1)) < (HEAD_DIM // 4)
    ca, sa, cb, sb = ca_ref[...], sa_ref[...], cb_ref[...], sb_ref[...]
    scale = HEAD_DIM ** -0.5 * LOG2_E

    def norm(a, g):
        return a * _rms(a, HEAD_DIM) * g

    def rope_a(a):
        partner = jnp.where(first_quarter, pltpu.roll(a, 96, 1), pltpu.roll(a, 32, 1))
        return a * ca + partner * sa

    def rope_b(a):
        return a * cb + pltpu.roll(a, 64, 1) * sb

    plan = _head_plan()
    chunk = 2 * HEAD_DIM
    for c in range(IN_COLS // chunk):
        acc = jnp.dot(h, w_ref[:, c * chunk:(c + 1) * chunk], preferred_element_type=F32)
        for half in range(2):
            kind, dst = plan[2 * c + half]
            a = acc[:, half * HEAD_DIM:(half + 1) * HEAD_DIM]
            if kind == "qa":
                a = rope_a(norm(a, qg_ref[...])) * scale
            elif kind == "ka":
                a = rope_a(norm(a, kg_ref[...]))
            elif kind == "qb":
                a = rope_b(a) * scale
            elif kind == "kb":
                a = rope_b(a)
            o_ref[:, dst * HEAD_DIM:(dst + 1) * HEAD_DIM] = a.astype(o_ref.dtype)


def _qkv(xf, mod_l, pre_g, w_in, qg, kg, tables, seq):
    t, d = xf.shape
    tm = min(512, seq)
    per_seq = seq // tm
    row = lambda i: (i, 0)
    tab = lambda i: (i % per_seq, 0)
    vec = pl.BlockSpec((1, d), lambda i: (0, 0))
    hvec = pl.BlockSpec((1, HEAD_DIM), lambda i: (0, 0))
    tspec = pl.BlockSpec((tm, HEAD_DIM), tab)
    return pl.pallas_call(
        _qkv_body,
        grid=(t // tm,),
        in_specs=[pl.BlockSpec((tm, d), row), vec,
                  pl.BlockSpec((1, 1, d), lambda i: ((i // per_seq) * 6 + 0, 0, 0)),
                  pl.BlockSpec((1, 1, d), lambda i: ((i // per_seq) * 6 + 1, 0, 0)),
                  _resident((d, IN_COLS), lambda i: (0, 0)),
                  hvec, hvec, tspec, tspec, tspec, tspec],
        out_specs=pl.BlockSpec((tm, IN_COLS), row),
        out_shape=jax.ShapeDtypeStruct((t, IN_COLS), BF16),
        compiler_params=_params("arbitrary"),
        name="qkv_proj",
    )(xf, pre_g.reshape(1, d), mod_l, mod_l, w_in, qg.reshape(1, HEAD_DIM), kg.reshape(1, HEAD_DIM),
      *tables)


ONES_ROWS = 16


def _values_t(v):
    vt = v.astype(F32).T.astype(BF16)
    return jnp.concatenate([vt, jnp.ones((ONES_ROWS, vt.shape[1]), BF16)], axis=0)


def _reduce_rows(fn, v):
    rows, n = v.shape
    if rows % 128 == 0 and rows > 128:
        v = fn(v.reshape(rows // 128, 128, n), axis=0)
    return fn(v, axis=0, keepdims=True)


def _attn_body(sinks_ref, qa_ref, qb_ref, kva_ref, kvb_ref, gna_ref, gnb_ref, o_ref,
               vt_ref, acc_ref, *, tq, seq, wk):
    i = pl.program_id(1)
    nt = (((1,), (1,)), ((), ()))

    @pl.when(i == 0)
    def _():
        for kv in range(N_KV):
            vt_ref[kv] = _values_t(kva_ref[:, KV_W + kv * HEAD_DIM:KV_W + (kv + 1) * HEAD_DIM])

    def stack_q(q_ref, hd0, n):
        qs = [q_ref[:, hd * HEAD_DIM:(hd + 1) * HEAD_DIM] for hd in range(hd0, hd0 + n)]
        return qs[0] if n == 1 else jnp.concatenate(qs, axis=0)

    def scores_a(hd0, n):
        k = kva_ref[:, (hd0 // GQA) * HEAD_DIM:(hd0 // GQA + 1) * HEAD_DIM]
        return lax.dot_general(k, stack_q(qa_ref, hd0, n), nt, preferred_element_type=F32)

    def finish_a(hd0, n, s):
        m = _reduce_rows(jnp.max, s)
        p = jnp.exp2(s - m).astype(BF16)
        o = jnp.dot(vt_ref[hd0 // GQA], p, preferred_element_type=F32)
        o = o[:HEAD_DIM] / o[HEAD_DIM:HEAD_DIM + 1]
        for j in range(n):
            acc_ref[hd0 + j] = o[:, j * tq:(j + 1) * tq]

    start = pl.multiple_of(jnp.clip(i * tq - WINDOW, 0, seq - wk), WINDOW)
    kpos = start + lax.broadcasted_iota(jnp.int32, (wk, tq), 0)
    qpos = i * tq + lax.broadcasted_iota(jnp.int32, (wk, tq), 1)
    bias = jnp.where(jnp.abs(kpos - qpos) <= WINDOW, 0.0, -jnp.inf)
    vt_b = [_values_t(kvb_ref[pl.ds(start, wk), KV_W + kv * HEAD_DIM:KV_W + (kv + 1) * HEAD_DIM])
            for kv in range(N_KV)]

    def scores_b(hd0, n):
        k = kvb_ref[pl.ds(start, wk), (hd0 // GQA) * HEAD_DIM:(hd0 // GQA + 1) * HEAD_DIM]
        s = lax.dot_general(k, stack_q(qb_ref, hd0, n), nt, preferred_element_type=F32)
        return s + (bias if n == 1 else jnp.concatenate([bias] * n, axis=1))

    def finish_b(hd0, n, s):
        sinks = [jnp.full((1, tq), sinks_ref[hd0 + j] * LOG2_E, F32) for j in range(n)]
        sink = sinks[0] if n == 1 else jnp.concatenate(sinks, axis=1)
        m = jnp.maximum(_reduce_rows(jnp.max, s), sink)
        e = jnp.exp2(s - m).astype(BF16)
        o = jnp.dot(vt_b[hd0 // GQA], e, preferred_element_type=F32)
        o = o[:HEAD_DIM] / (o[HEAD_DIM:HEAD_DIM + 1] + jnp.exp2(sink - m))
        for j in range(n):
            acc_ref[N_Q + hd0 + j] = o[:, j * tq:(j + 1) * tq]

    jobs = ([(scores_a, finish_a, hd0, HEADS_PER_JOB_A) for hd0 in range(0, N_Q, HEADS_PER_JOB_A)]
            + [(scores_b, finish_b, hd0, HEADS_PER_JOB_B) for hd0 in range(0, N_Q, HEADS_PER_JOB_B)])
    ahead = min(JOBS_AHEAD, len(jobs))
    pending = [jobs[j][0](jobs[j][2], jobs[j][3]) for j in range(ahead)]
    for j, (_, finish, hd0, n) in enumerate(jobs):
        if j + ahead < len(jobs):
            nxt = jobs[j + ahead]
            pending.append(nxt[0](nxt[2], nxt[3]))
        finish(hd0, n, pending.pop(0))

    for grp, gain_ref in enumerate((gna_ref, gnb_ref)):
        ss = jnp.zeros((1, tq), F32)
        for hd in range(N_Q):
            o = acc_ref[grp * N_Q + hd]
            ss = ss + jnp.sum(o * o, axis=0, keepdims=True)
        r = lax.rsqrt(ss * (1.0 / Q_W) + EPS)
        for hd in range(N_Q):
            col = grp * Q_W + hd * HEAD_DIM
            o = acc_ref[grp * N_Q + hd] * r * gain_ref[hd]
            o_ref[:, col:col + HEAD_DIM] = o.astype(o_ref.dtype).T


def _attention(qkv, sinks, gna, gnb, batch, seq):
    t = qkv.shape[0]
    tq = min(256, seq)
    wk = min(tq + 2 * WINDOW, seq)
    per_seq = seq // tq
    row = lambda b, i: (b * per_seq + i, 0)
    body = functools.partial(_attn_body, tq=tq, seq=seq, wk=wk)
    gvec = pl.BlockSpec((N_Q, HEAD_DIM, 1), lambda b, i: (0, 0, 0))
    return pl.pallas_call(
        body,
        grid=(batch, per_seq),
        in_specs=[pl.BlockSpec(memory_space=pltpu.SMEM),
                  pl.BlockSpec((tq, Q_W), lambda b, i: (b * per_seq + i, 0)),
                  pl.BlockSpec((tq, Q_W), lambda b, i: (b * per_seq + i, 1)),
                  pl.BlockSpec((seq, 2 * KV_W), lambda b, i: (b, 4)),
                  pl.BlockSpec((seq, 2 * KV_W), lambda b, i: (b, 5)),
                  gvec, gvec],
        out_specs=pl.BlockSpec((tq, 2 * Q_W), row),
        out_shape=jax.ShapeDtypeStruct((t, 2 * Q_W), BF16),
        scratch_shapes=[pltpu.VMEM((N_KV, HEAD_DIM + ONES_ROWS, seq), BF16),
                        pltpu.VMEM((2 * N_Q, HEAD_DIM, tq), F32)],
        compiler_params=_params("arbitrary", "arbitrary"),
        name="attention",
    )(sinks, qkv, qkv, qkv, qkv, gna.reshape(N_Q, HEAD_DIM, 1), gnb.reshape(N_Q, HEAD_DIM, 1))


def _outproj_body(*refs, router):
    if router:
        (m_ref, w_ref, x_ref, pg_ref, g1_ref, fg_ref, sh2_ref, sc2_ref, wr_ref,
         x1_ref, h2_ref, rout_ref, ridx_ref) = refs
    else:
        m_ref, w_ref, x_ref, pg_ref, g1_ref, fg_ref, sh2_ref, sc2_ref, x1_ref, h2_ref = refs
    tm, d = x_ref.shape
    rows = min(tm, OUTPROJ_ROWS)

    def project(r):
        return jnp.dot(m_ref[r * rows:(r + 1) * rows, :], w_ref[...], preferred_element_type=F32)

    def epilogue(r, y):
        sl = slice(r * rows, (r + 1) * rows)
        x1 = x_ref[sl, :] + g1_ref[0] * (y * _rms(y, d) * pg_ref[...])
        x1_ref[sl, :] = x1
        h2 = (x1 * _rms(x1, d) * fg_ref[...]) * (1.0 + sc2_ref[0]) + sh2_ref[0]
        h2_ref[sl, :] = h2.astype(h2_ref.dtype)
        if router:
            logits = jnp.dot(h2.astype(BF16), wr_ref[...], preferred_element_type=F32)
            lane = lax.broadcasted_iota(jnp.int32, logits.shape, 1).astype(F32)
            lg = jnp.where(lane < N_EXPERTS, logits, -jnp.inf)
            m1 = jnp.max(lg, axis=-1, keepdims=True)
            i1 = jnp.min(jnp.where(lg == m1, lane, float(LANES)), axis=-1, keepdims=True)
            lg2 = jnp.where(lane == i1, -jnp.inf, lg)
            m2 = jnp.max(lg2, axis=-1, keepdims=True)
            i2 = jnp.min(jnp.where(lg2 == m2, lane, float(LANES)), axis=-1, keepdims=True)
            e2 = jnp.exp(m2 - m1)
            den = 1.0 + e2
            rout = jnp.where(lane == 0, i1, jnp.where(lane == 1, i2, jnp.where(
                lane == 2, 1.0 / den, jnp.where(lane == 3, e2 / den, 0.0))))
            rout_ref[sl, :] = rout
            ridx_ref[:, sl] = rout.T[:8, :]

    y = project(0)
    for r in range(tm // rows):
        y_next = project(r + 1) if r + 1 < tm // rows else None
        epilogue(r, y)
        y = y_next


def _outproj(merged, w_out, xf, post_g, ffn_g, mod_l, seq, w_router=None):
    t, d = xf.shape
    k = merged.shape[1]
    tm = min(512, seq)
    per_seq = seq // tm
    row = lambda i: (i, 0)
    vec = pl.BlockSpec((1, d), lambda i: (0, 0))
    modv = lambda j: pl.BlockSpec((1, 1, d), lambda i: ((i // per_seq) * 6 + j, 0, 0))
    router = w_router is not None
    in_specs = [pl.BlockSpec((tm, k), row), _resident((k, d), lambda i: (0, 0)),
                pl.BlockSpec((tm, d), row), vec, modv(2), vec, modv(3), modv(4)]
    args = [merged, w_out, xf, post_g.reshape(1, d), mod_l, ffn_g.reshape(1, d), mod_l, mod_l]
    out_specs = [pl.BlockSpec((tm, d), row), pl.BlockSpec((tm, d), row)]
    out_shape = [jax.ShapeDtypeStruct((t, d), F32),
                 jax.ShapeDtypeStruct((t, d), F32 if router else BF16)]
    if router:
        wr = jnp.zeros((d, LANES), BF16).at[:, :N_EXPERTS].set(w_router.astype(BF16))
        in_specs.append(_resident((d, LANES), lambda i: (0, 0)))
        args.append(wr)
        out_specs += [pl.BlockSpec((tm, LANES), row), pl.BlockSpec((8, tm), lambda i: (0, i))]
        out_shape += [jax.ShapeDtypeStruct((t, LANES), F32), jax.ShapeDtypeStruct((8, t), F32)]
    return pl.pallas_call(
        functools.partial(_outproj_body, router=router),
        grid=(t // tm,),
        in_specs=in_specs, out_specs=out_specs, out_shape=out_shape,
        compiler_params=_params("arbitrary"),
        name="out_proj_router" if router else "out_proj",
    )(*args)


def _ring_copies(srcs, bufs, sems, slot):
    return [pltpu.make_async_copy(src, buf.at[slot], sems.at[k, slot])
            for k, (src, buf) in enumerate(zip(srcs, bufs))]


def _ffn_body(h_ref, wg_hbm, wu_hbm, wd_hbm, x1_ref, g2_ref, pg_ref, *rest, nf, n_steps, tf, n_cast,
              ring=DENSE_RING):
    cast_in, o_ref = rest[:n_cast], rest[n_cast]
    cast_out = rest[n_cast + 1:2 * n_cast + 1]
    wg_buf, wu_buf, wd_buf, wsem = rest[2 * n_cast + 1:]
    f = pl.program_id(1)
    step = pl.program_id(0) * nf + f
    slot = lax.rem(step, ring)

    def chunk_copies(chunk, dst_slot):
        cols = pl.ds(pl.multiple_of(chunk * tf, tf), tf)
        return _ring_copies((wg_hbm.at[:, cols], wu_hbm.at[:, cols], wd_hbm.at[cols, :]),
                            (wg_buf, wu_buf, wd_buf), wsem, dst_slot)

    @pl.when(step == 0)
    def _():
        for k in range(ring - 1):
            for c in chunk_copies(k % nf, k):
                c.start()

    @pl.when(step + (ring - 1) < n_steps)
    def _():
        ahead = step + (ring - 1)
        for c in chunk_copies(lax.rem(ahead, nf), lax.rem(ahead, ring)):
            c.start()

    for c in chunk_copies(f, slot):
        c.wait()

    @pl.when(f == 0)
    def _():
        o_ref[...] = jnp.zeros_like(o_ref)

    for src, dst in zip(cast_in, cast_out):
        dst[...] = src[...].astype(dst.dtype)

    h = h_ref[...]
    g = jnp.dot(h, wg_buf[slot], preferred_element_type=F32)
    u = jnp.dot(h, wu_buf[slot], preferred_element_type=F32)
    a = (_silu(g) * u).astype(BF16)
    o_ref[...] += jnp.dot(a, wd_buf[slot], preferred_element_type=F32)

    @pl.when(f == nf - 1)
    def _():
        y = o_ref[...]
        o_ref[...] = x1_ref[...] + g2_ref[0] * (y * _rms(y, y.shape[-1]) * pg_ref[...])


def _weight_ring_scratch(d, tf, ring):
    return [pltpu.VMEM((ring, d, tf), BF16), pltpu.VMEM((ring, d, tf), BF16),
            pltpu.VMEM((ring, tf, d), BF16), pltpu.SemaphoreType.DMA((3, ring))]


def _ffn_tile(f_dim, prefer=(512, 256, 128)):
    for tf in prefer:
        if f_dim % tf == 0:
            return tf
    return f_dim


def _cast_spec(arr, n_i, nf):
    rows, cols = arr.shape
    if rows % n_i == 0 and cols % nf == 0 and (rows // n_i) % 16 == 0 and (cols // nf) % LANES == 0:
        return pl.BlockSpec((rows // n_i, cols // nf), lambda i, f: (i, f))
    if rows % (n_i * nf) == 0 and (rows // (n_i * nf)) % 16 == 0:
        return pl.BlockSpec((rows // (n_i * nf), cols), lambda i, f: (i * nf + f, 0))
    return None


def _dense_ffn(h2, wg, wu, wd, x1, post_g, mod_l, seq, to_cast=()):
    t, d = x1.shape
    f_dim = wg.shape[1]
    tm = min(512, seq)
    tf = _ffn_tile(f_dim)
    nf = f_dim // tf
    n_i = t // tm
    per_seq = seq // tm
    row = lambda i, f: (i, 0)
    cast_specs = [_cast_spec(a, t // tm, nf) for a in to_cast]
    hosted = [a for a, sp in zip(to_cast, cast_specs) if sp is not None]
    specs = [sp for sp in cast_specs if sp is not None]
    outs = pl.pallas_call(
        functools.partial(_ffn_body, nf=nf, n_steps=n_i * nf, tf=tf, n_cast=len(hosted)),
        grid=(n_i, nf),
        in_specs=[pl.BlockSpec((tm, d), row),
                  pl.BlockSpec(memory_space=pl.ANY),
                  pl.BlockSpec(memory_space=pl.ANY),
                  pl.BlockSpec(memory_space=pl.ANY),
                  pl.BlockSpec((tm, d), row),
                  pl.BlockSpec((1, 1, d), lambda i, f: ((i // per_seq) * 6 + 5, 0, 0)),
                  pl.BlockSpec((1, d), lambda i, f: (0, 0))] + specs,
        out_specs=[pl.BlockSpec((tm, d), row)] + specs,
        out_shape=[jax.ShapeDtypeStruct((t, d), F32)]
        + [jax.ShapeDtypeStruct(a.shape, BF16) for a in hosted],
        scratch_shapes=_weight_ring_scratch(d, tf, DENSE_RING),
        compiler_params=_params("arbitrary", "arbitrary"),
        name="dense_ffn",
    )(h2, wg, wu, wd, x1, mod_l, post_g.reshape(1, d), *hosted)
    done = iter(outs[1:])
    return outs[0], [next(done) if sp is not None else a.astype(BF16)
                     for a, sp in zip(to_cast, cast_specs)]


def _wait_slot(buf_ref, sem):
    pltpu.make_async_copy(buf_ref, buf_ref, sem).wait()


def _moe_body(te_ref, nu_ref, rows_ref, nrows_ref, h_hbm, wg_hbm, wu_hbm, wd_hbm, y_ref,
              xg_ref, hb_ref, sem, wg_buf, wu_buf, wd_buf, wsem, *, tm, tf, cs, nc, nf,
              ring=EXPERT_RING):
    i = pl.program_id(0)
    f = pl.program_id(1)
    n_used = nu_ref[0]
    used = i < n_used
    slot = lax.rem(i, 2)
    step = i * nf + f
    n_steps = n_used * nf
    wslot = lax.rem(step, ring)

    def chunk_copies(tile, chunk, dst_slot):
        e = te_ref[tile]
        cols = pl.ds(pl.multiple_of(chunk * tf, tf), tf)
        return _ring_copies((wg_hbm.at[e, :, cols], wu_hbm.at[e, :, cols], wd_hbm.at[e, cols, :]),
                            (wg_buf, wu_buf, wd_buf), wsem, dst_slot)

    @pl.when(step == 0)
    def _():
        for k in range(ring - 1):
            for c in chunk_copies(k // nf, k % nf, k):
                c.start(priority=1)

    @pl.when(step + (ring - 1) < n_steps)
    def _():
        ahead = step + (ring - 1)
        tile = ahead // nf
        for c in chunk_copies(tile, ahead - tile * nf, lax.rem(ahead, ring)):
            c.start(priority=1)

    @pl.when(used)
    def _():
        for c in chunk_copies(i, f, wslot):
            c.wait()

    def start_row(idx_ref, chunk, j, dst_slot):
        tok = idx_ref[0, 0, chunk * cs + j]
        pltpu.make_async_copy(h_hbm.at[pl.ds(tok, 1)], xg_ref.at[dst_slot, chunk, pl.ds(j, 1)],
                              sem.at[dst_slot]).start()

    @pl.when((i == 0) & (f == 0))
    def _():
        def chunk_body(c, carry):
            for j in range(cs):
                start_row(rows_ref, c, j, 0)
            return carry
        lax.fori_loop(0, nc, chunk_body, 0)

    @pl.when((i <= n_used) & (f == 0))
    def _():
        _wait_slot(xg_ref.at[slot], sem.at[slot])

    @pl.when(used & (f == 0))
    def _():
        for c in range(nc):
            n = min(cs, tm - c * cs)
            if n > 0:
                hb_ref[c * cs:c * cs + n, :] = xg_ref[slot, c, :n, :].astype(BF16)
        y_ref[...] = jnp.zeros_like(y_ref)

    def ffn_step(issue):
        if issue:
            for j in range(cs):
                start_row(nrows_ref, f, j, 1 - slot)
        h = hb_ref[...]
        g = jnp.dot(h, wg_buf[wslot], preferred_element_type=F32)
        u = jnp.dot(h, wu_buf[wslot], preferred_element_type=F32)
        a = (_silu(g) * u).astype(BF16)
        y_ref[...] += jnp.dot(a, wd_buf[wslot], preferred_element_type=F32)

    pl.when(used & (f < nc))(functools.partial(ffn_step, True))
    if nc < nf:
        pl.when(used & (f >= nc))(functools.partial(ffn_step, False))

    @pl.when(jnp.logical_not(used) & (f == nf - 1))
    def _():
        y_ref[...] = jnp.zeros_like(y_ref)


def _combine_body(p0_ref, p1_ref, np0_ref, np1_ref, y_hbm, rout_ref, x1_ref, g2_ref, pg_ref, o_ref,
                  yb_ref, sem, *, tc, n_steps):
    i = pl.program_id(0)
    slot = lax.rem(i, 2)
    d = o_ref.shape[-1]

    def start_rows(idx_refs, dst_slot):
        def body(it, carry):
            for j in range(8):
                for k in range(2):
                    p = idx_refs[k][0, 0, it * 8 + j]
                    pltpu.make_async_copy(y_hbm.at[pl.ds(p, 1)],
                                          yb_ref.at[dst_slot, k, it, pl.ds(j, 1)],
                                          sem.at[dst_slot]).start(priority=k)
            return carry
        lax.fori_loop(0, tc // 8, body, 0)

    @pl.when(i == 0)
    def _():
        start_rows((p0_ref, p1_ref), 0)

    @pl.when(i + 1 < n_steps)
    def _():
        start_rows((np0_ref, np1_ref), 1 - slot)

    _wait_slot(yb_ref.at[slot], sem.at[slot])
    w = rout_ref[...]
    y = (w[:, 2:3] * yb_ref[slot, 0].reshape(tc, d) + w[:, 3:4] * yb_ref[slot, 1].reshape(tc, d))
    o_ref[...] = x1_ref[...] + g2_ref[0] * (y * _rms(y, d) * pg_ref[...])


def _moe_ffn(h2, rout, ridx, wg, wu, wd, x1, post_g, mod_l, seq):
    t, d = x1.shape
    f_dim = wg.shape[2]
    tm = min(512, seq)
    tf = _ffn_tile(f_dim)
    nf = f_dim // tf
    n_pairs = 2 * t
    n_tiles = n_pairs // tm + N_EXPERTS + 1
    n_rows = n_tiles * tm
    nc = min(nf, 8)
    cs = -(-tm // (nc * 16)) * 16
    tg = nc * cs

    expert = ridx[:2].astype(jnp.int32).reshape(1, n_pairs)
    onehot = (expert == jnp.arange(N_EXPERTS, dtype=jnp.int32)[:, None]).astype(jnp.int32)
    blocks = n_pairs // LANES
    tri = (jnp.arange(LANES)[:, None] <= jnp.arange(LANES)[None, :]).astype(F32)
    within = jnp.dot(onehot.reshape(N_EXPERTS * blocks, LANES).astype(F32), tri,
                     preferred_element_type=F32).astype(jnp.int32).reshape(N_EXPERTS, blocks, LANES)
    block_tot = within[:, :, -1]
    csum = (within + (jnp.cumsum(block_tot, axis=1) - block_tot)[:, :, None]).reshape(N_EXPERTS, n_pairs)
    rank = jnp.sum(onehot * csum, axis=0) - 1
    tiles_per = (csum[:, -1] + tm - 1) // tm
    tile_end = jnp.cumsum(tiles_per)
    tile_start = tile_end - tiles_per
    n_used = tile_end[-1]
    pos = jnp.sum(onehot * tile_start[:, None], axis=0) * tm + rank
    token = jnp.tile(jnp.arange(t, dtype=jnp.int32), 2)
    rows = jnp.zeros((n_rows,), jnp.int32).at[pos].set(token)
    tile_ids = jnp.minimum(jnp.arange(n_tiles, dtype=jnp.int32), n_used - 1)
    tile_expert = jnp.sum((tile_ids[:, None] >= tile_end[None, :]).astype(jnp.int32), axis=1)

    rows = jnp.pad(rows.reshape(n_tiles, tm), ((0, 0), (0, tg - tm))).reshape(n_tiles, 1, tg)

    y_sorted = pl.pallas_call(
        functools.partial(_moe_body, tm=tm, tf=tf, cs=cs, nc=nc, nf=nf),
        grid_spec=pltpu.PrefetchScalarGridSpec(
            num_scalar_prefetch=2,
            grid=(n_tiles, nf),
            in_specs=[pl.BlockSpec((1, 1, tg), lambda i, f, te, nu: (i, 0, 0),
                                   memory_space=pltpu.SMEM),
                      pl.BlockSpec((1, 1, tg),
                                   lambda i, f, te, nu: (jnp.minimum(i + 1, n_tiles - 1), 0, 0),
                                   memory_space=pltpu.SMEM),
                      pl.BlockSpec(memory_space=pl.ANY),
                      pl.BlockSpec(memory_space=pl.ANY),
                      pl.BlockSpec(memory_space=pl.ANY),
                      pl.BlockSpec(memory_space=pl.ANY)],
            out_specs=pl.BlockSpec((tm, d), lambda i, f, te, nu: (i, 0)),
            scratch_shapes=[pltpu.VMEM((2, nc, cs, d), F32), pltpu.VMEM((tm, d), BF16),
                            pltpu.SemaphoreType.DMA((2,))]
            + _weight_ring_scratch(d, tf, EXPERT_RING)),
        out_shape=jax.ShapeDtypeStruct((n_rows, d), F32),
        compiler_params=_params("arbitrary", "arbitrary"),
        name="expert_ffn",
    )(tile_expert, n_used.reshape(1).astype(jnp.int32), rows, rows, h2, wg, wu, wd)

    tc = min(256, seq)
    per_seq = seq // tc
    n_steps = t // tc
    row = lambda i: (i, 0)
    pos = pos.reshape(2, n_steps, 1, tc)
    cur = pl.BlockSpec((1, 1, tc), lambda i: (i, 0, 0), memory_space=pltpu.SMEM)
    nxt = pl.BlockSpec((1, 1, tc), lambda i: (jnp.minimum(i + 1, n_steps - 1), 0, 0),
                       memory_space=pltpu.SMEM)
    return pl.pallas_call(
        functools.partial(_combine_body, tc=tc, n_steps=n_steps),
        grid=(n_steps,),
        in_specs=[cur, cur, nxt, nxt,
                  pl.BlockSpec(memory_space=pl.ANY),
                  pl.BlockSpec((tc, LANES), row),
                  pl.BlockSpec((tc, d), row),
                  pl.BlockSpec((1, 1, d), lambda i: ((i // per_seq) * 6 + 5, 0, 0)),
                  pl.BlockSpec((1, d), lambda i: (0, 0))],
        out_specs=pl.BlockSpec((tc, d), row),
        out_shape=jax.ShapeDtypeStruct((t, d), F32),
        scratch_shapes=[pltpu.VMEM((2, 2, tc // 8, 8, d), F32), pltpu.SemaphoreType.DMA((2,))],
        compiler_params=_params("arbitrary"),
        name="expert_combine",
    )(pos[0], pos[1], pos[0], pos[1], y_sorted, rout, x1, mod_l, post_g.reshape(1, d))


def kernel(x, c, w_mod, b_mod, pre_mix_g, post_mix_g, pre_ffn_g, post_ffn_g, w_in, q_norm_g, k_norm_g, sinks, grp_norm_a, grp_norm_b, w_out, w_ffn_gate, w_ffn_up, w_ffn_down, w_router, w_exp_gate, w_exp_up, w_exp_down):
    batch, seq, d = x.shape
    n_layers = w_mod.shape[0]
    assert seq % 256 == 0 or seq < 256, seq
    assert w_in.shape[2] == IN_COLS and w_out.shape[1] == 2 * Q_W

    mod = _modulation(c, w_mod, b_mod)
    tables = _rope_tables(seq)
    xf = x.reshape(batch * seq, d)
    expert_w = None
    for l in range(n_layers):
        mod_l = mod[l].reshape(batch * 6, 1, d)
        qkv = _qkv(xf, mod_l, pre_mix_g[l], w_in[l].astype(BF16), q_norm_g[l], k_norm_g[l],
                   tables, seq)
        merged = _attention(qkv, sinks[l], grp_norm_a[l], grp_norm_b[l], batch, seq)
        i = l // 2
        if l % 2 == 0:
            x1, h2 = _outproj(merged, w_out[l].astype(BF16), xf, post_mix_g[l], pre_ffn_g[l],
                              mod_l, seq)
            to_cast = ()
            if l + 1 < n_layers:
                to_cast = [w.reshape(-1, w.shape[-1])
                           for w in (w_exp_gate[i], w_exp_up[i], w_exp_down[i])]
            xf, cast = _dense_ffn(h2, w_ffn_gate[i].astype(BF16), w_ffn_up[i].astype(BF16),
                                  w_ffn_down[i].astype(BF16), x1, post_ffn_g[l], mod_l, seq,
                                  to_cast=to_cast)
            if cast:
                expert_w = [c2.reshape(w.shape[1:]) for c2, w in
                            zip(cast, (w_exp_gate, w_exp_up, w_exp_down))]
        else:
            x1, h2, rout, ridx = _outproj(merged, w_out[l].astype(BF16), xf, post_mix_g[l],
                                    pre_ffn_g[l], mod_l, seq, w_router=w_router[i])
            if expert_w is None:
                expert_w = [w[i].astype(BF16) for w in (w_exp_gate, w_exp_up, w_exp_down)]
            xf = _moe_ffn(h2, rout, ridx, *expert_w, x1, post_ffn_g[l], mod_l, seq)
            expert_w = None
    return xf.reshape(batch, seq, d)
```
